```python
import math
import jax
import jax.numpy as jnp
from jax import lax
import numpy as np

D_MODEL = 2048
BATCH = 16
SEQ = 256
DEPTH = 1
DEC_BATCH = 2
DEC_SEQ = 4096
PAST_LEN = 256

GRID_W = 64
EPS = 1e-6
N_MOD = 6

D_HYENA = 1024
HYENA_ORDER = 2
SHORT_CONV = 3
FILTER_BANDS = 16
FILTER_EMB = 1 + 2 * FILTER_BANDS
FILTER_HIDDEN = 64
FILTER_OUT = HYENA_ORDER * 2 * D_HYENA
DECAY_TARGET = 1e-2
SHORT_DECAY_PCT = 0.3
LONG_DECAY_PCT = 1.5

N_HEADS = 16
QK_NOPE = 128
QK_ROPE = 64
QK_HEAD = QK_NOPE + QK_ROPE
V_HEAD = 128
Q_LORA = 512
KV_LORA = 256
ROPE_THETA = 10000.0
Q_BLOCK = 128

N_EXPERTS = 32
TOP_K = 4
D_EXPERT = 2048
SWIGLU_LIMIT = 7.0
SWIGLU_ALPHA = 1.702
MOE_BLOCK = 128

IN_COLS = 3 * D_HYENA + Q_LORA + KV_LORA + QK_ROPE + 2 * D_MODEL
IN_SPLITS = (3 * D_HYENA,
             3 * D_HYENA + Q_LORA,
             3 * D_HYENA + Q_LORA + KV_LORA,
             3 * D_HYENA + Q_LORA + KV_LORA + QK_ROPE,
             3 * D_HYENA + Q_LORA + KV_LORA + QK_ROPE + D_MODEL)

kernel_name = 'hyena_mla_moe_diffusion_step'


def rms_norm(x, w):
    xf = x.astype(jnp.float32)
    y = xf * lax.rsqrt(jnp.mean(xf * xf, axis=-1, keepdims=True) + EPS)
    return (y * w.astype(jnp.float32)).astype(x.dtype)


def modulation(cvec, w_mod, b_mod):
    mod = jax.nn.silu(cvec) @ w_mod + b_mod
    return jnp.split(mod, N_MOD, axis=-1)


def axial_rope(n_tokens):
    rows = n_tokens // GRID_W
    row = jnp.repeat(jnp.arange(rows, dtype=jnp.float32), GRID_W)
    col = jnp.tile(jnp.arange(GRID_W, dtype=jnp.float32), rows)
    n_freq = QK_ROPE // 4
    inv_freq = jnp.power(ROPE_THETA, -jnp.arange(n_freq, dtype=jnp.float32) / n_freq)
    ang = jnp.concatenate([row[:, None] * inv_freq, col[:, None] * inv_freq], axis=-1)
    ang = jnp.concatenate([ang, ang], axis=-1)
    return jnp.cos(ang), jnp.sin(ang)


def apply_rope(x, cos, sin):
    xf = x.astype(jnp.float32)
    half = QK_ROPE // 2
    rot = jnp.concatenate([-xf[..., half:], xf[..., :half]], axis=-1)
    return (xf * cos + rot * sin).astype(x.dtype)


def centred_short_conv(u, w, b):
    n = u.shape[1]
    pad = SHORT_CONV // 2
    up = jnp.pad(u, ((0, 0), (pad, pad), (0, 0)))
    out = b
    for j in range(SHORT_CONV):
        out = out + up[:, j:j + n] * w[j]
    return out


def implicit_filters(n, w1, b1, w2, b2, w3, b3, freq):
    f32 = jnp.float32
    t = jnp.linspace(0.0, 1.0, n, dtype=f32)[:, None]
    w = (2.0 * math.pi / n) * jnp.arange(n, dtype=f32)[:, None]
    bands = jnp.linspace(1e-4, FILTER_BANDS - 1, FILTER_BANDS, dtype=f32)[None, :]
    feats = jnp.concatenate([t, jnp.cos(bands * w), -jnp.sin(bands * w)], axis=-1)
    freq = freq.astype(f32)
    hdn = jnp.sin(freq[0] * (feats @ w1.astype(f32) + b1.astype(f32)))
    hdn = jnp.sin(freq[1] * (hdn @ w2.astype(f32) + b2.astype(f32)))
    filt = hdn @ w3.astype(f32) + b3.astype(f32)
    max_decay = math.log(DECAY_TARGET) / SHORT_DECAY_PCT
    min_decay = math.log(DECAY_TARGET) / LONG_DECAY_PCT
    deltas = jnp.linspace(min_decay, max_decay, D_HYENA, dtype=f32)
    deltas = jnp.tile(deltas, HYENA_ORDER * 2)
    return filt * jnp.exp(-t * jnp.abs(deltas)[None, :])


def bidirectional_long_conv(z, h_fwd, h_bwd, skip):
    n = z.shape[1]
    k = jnp.concatenate([h_fwd, jnp.zeros_like(h_fwd[:1]), jnp.flip(h_bwd[:n - 1], axis=0)], axis=0)
    zf = jnp.fft.rfft(z.astype(jnp.float32), n=2 * n, axis=1)
    kf = jnp.fft.rfft(k, n=2 * n, axis=0)
    y = jnp.fft.irfft(zf * kf[None], n=2 * n, axis=1)[:, :n]
    return (y + z.astype(jnp.float32) * skip.astype(jnp.float32)).astype(z.dtype)


def hyena_operator(u3, lp):
    n = u3.shape[1]
    u3 = centred_short_conv(u3, lp['hy_conv_w'], lp['hy_conv_b'])
    v, x1, x2 = jnp.split(u3, 3, axis=-1)
    h = implicit_filters(n, lp['filt_w1'], lp['filt_b1'], lp['filt_w2'], lp['filt_b2'],
                         lp['filt_w3'], lp['filt_b3'], lp['filt_freq'])
    h = h.reshape(n, HYENA_ORDER, 2, D_HYENA)
    z = v
    for o, gate in enumerate((x1, x2)):
        z = gate * bidirectional_long_conv(z, h[:, o, 0], h[:, o, 1], lp['hy_skip'][o])
    return z


def mla_attention(q_nope, q_pe, k_nope, k_pe, v):
    b, lq = q_nope.shape[:2]
    nb = lq // Q_BLOCK
    scale = QK_HEAD ** -0.5
    qn = q_nope.reshape(b, nb, Q_BLOCK, N_HEADS, QK_NOPE).transpose(1, 0, 2, 3, 4)
    qp = q_pe.reshape(b, nb, Q_BLOCK, N_HEADS, QK_ROPE).transpose(1, 0, 2, 3, 4)

    def block(args):
        qn_b, qp_b = args
        s = (jnp.einsum('bqhd,bkhd->bhqk', qn_b, k_nope, preferred_element_type=jnp.float32)
             + jnp.einsum('bqhr,bkr->bhqk', qp_b, k_pe, preferred_element_type=jnp.float32))
        p = jax.nn.softmax(s * scale, axis=-1).astype(v.dtype)
        return jnp.einsum('bhqk,bkhd->bqhd', p, v)

    o = lax.map(block, (qn, qp))
    return o.transpose(1, 0, 2, 3, 4).reshape(b, lq, N_HEADS * V_HEAD)


def token_mixer(h, lp, rope, ctx_kv):
    b, n, _ = h.shape
    proj = jnp.einsum('bld,dc->blc', h, lp['w_in'])
    u3, q_c, kv_c, kpe_raw, g_hy, g_mla = jnp.split(proj, IN_SPLITS, axis=-1)
    y_hy = hyena_operator(u3, lp)
    q = (rms_norm(q_c, lp['q_a_norm']) @ lp['w_uq']).reshape(b, n, N_HEADS, QK_HEAD)
    q_nope = rms_norm(q[..., :QK_NOPE], lp['qn_norm'])
    q_pe = rms_norm(q[..., QK_NOPE:], lp['qr_norm'])
    c_kv = rms_norm(kv_c, lp['kv_a_norm'])
    k_pe = rms_norm(kpe_raw, lp['kr_norm'])
    if rope is not None:
        cos, sin = rope
        q_pe = apply_rope(q_pe, cos[:, None, :], sin[:, None, :])
        k_pe = apply_rope(k_pe, cos, sin)
    if ctx_kv is None:
        ckv_all, kpe_all = c_kv, k_pe
    else:
        ckv_all = jnp.concatenate([c_kv, ctx_kv[0]], axis=1)
        kpe_all = jnp.concatenate([k_pe, ctx_kv[1]], axis=1)
    kv = (ckv_all @ lp['w_ukv']).reshape(b, ckv_all.shape[1], N_HEADS, QK_NOPE + V_HEAD)
    k_nope = rms_norm(kv[..., :QK_NOPE], lp['kn_norm'])
    v = kv[..., QK_NOPE:]
    y_mla = mla_attention(q_nope, q_pe, k_nope, kpe_all, v)
    merged = (jax.nn.sigmoid(g_hy) * (y_hy @ lp['w_hy_out'])
              + jax.nn.sigmoid(g_mla) * (y_mla @ lp['w_mla_out']))
    return merged @ lp['w_o'], c_kv, k_pe


def moe_ffn(x, w_router, b_router, w_gate_up, b_gate_up, w_down, b_down):
    b, n, d = x.shape
    t = b * n
    xt = x.reshape(t, d)
    logits = jnp.einsum('td,de->te', xt, w_router, preferred_element_type=jnp.float32) + b_router.astype(jnp.float32)
    top_logits, top_idx = lax.top_k(logits, TOP_K)
    top_w = jax.nn.softmax(top_logits, axis=-1)
    n_assign = t * TOP_K
    flat_e = top_idx.reshape(n_assign)
    flat_tok = jnp.arange(n_assign, dtype=jnp.int32) // TOP_K
    flat_w = top_w.reshape(n_assign)
    order = jnp.argsort(flat_e)
    e_sorted = flat_e[order]
    counts = jnp.zeros((N_EXPERTS,), jnp.int32).at[flat_e].add(1)
    padded = (counts + MOE_BLOCK - 1) // MOE_BLOCK * MOE_BLOCK
    pad_end = jnp.cumsum(padded)
    pad_start = pad_end - padded
    grp_start = jnp.cumsum(counts) - counts
    dest = pad_start[e_sorted] + jnp.arange(n_assign, dtype=jnp.int32) - grp_start[e_sorted]
    n_blocks = -(-n_assign // MOE_BLOCK) + N_EXPERTS
    cap = n_blocks * MOE_BLOCK
    slot_tok = jnp.full((cap,), t, jnp.int32).at[dest].set(flat_tok[order])
    slot_w = jnp.zeros((cap,), jnp.float32).at[dest].set(flat_w[order])
    block_start = jnp.arange(n_blocks, dtype=jnp.int32) * MOE_BLOCK
    block_expert = jnp.minimum(jnp.searchsorted(pad_end, block_start, side='right'), N_EXPERTS - 1)
    x_pad = jnp.concatenate([xt, jnp.zeros((1, d), xt.dtype)], axis=0)

    def expert_block(args):
        tok, e = args
        xb = x_pad[tok]
        gu = xb @ w_gate_up[e] + b_gate_up[e]
        gate = jnp.minimum(gu[:, 0::2], SWIGLU_LIMIT)
        up = jnp.clip(gu[:, 1::2], -SWIGLU_LIMIT, SWIGLU_LIMIT)
        hid = gate * jax.nn.sigmoid(SWIGLU_ALPHA * gate) * (up + 1.0)
        return hid @ w_down[e] + b_down[e]

    out = lax.map(expert_block, (slot_tok.reshape(n_blocks, MOE_BLOCK), block_expert))
    out = out.reshape(cap, d) * slot_w[:, None].astype(out.dtype)
    y = jax.ops.segment_sum(out, slot_tok, num_segments=t + 1)[:t]
    return y.reshape(b, n, d)


def trunk_layer(x, mod, lp, rope, ctx_kv):
    shift1, scale1, gate1, shift2, scale2, gate2 = mod
    h = rms_norm(x, lp['norm1']) * (1.0 + scale1[:, None, :]) + shift1[:, None, :]
    mix, c_kv, k_pe = token_mixer(h, lp, rope, ctx_kv)
    x = x + gate1[:, None, :] * mix
    h = rms_norm(x, lp['norm2']) * (1.0 + scale2[:, None, :]) + shift2[:, None, :]
    x = x + gate2[:, None, :] * moe_ffn(h, lp['w_router'], lp['b_router'], lp['w_gate_up'],
                                        lp['b_gate_up'], lp['w_down'], lp['b_down'])
    return x, c_kv, k_pe


def setup_inputs(seed: int = 0) -> dict:
    key = jax.random.key(seed)
    ks = jax.random.split(key, 38)
    f32 = jnp.float32

    def nrm(k, shape, std):
        return std * jax.random.normal(k, shape, f32)

    def gain(k, shape):
        return 1.0 + 0.05 * jax.random.normal(k, shape, f32)

    return {
        'x_prompt': nrm(ks[0], (BATCH, SEQ, D_MODEL), 1.0),
        'x_sample': nrm(ks[1], (DEC_BATCH, DEC_SEQ, D_MODEL), 1.0),
        'cache_ckv': nrm(ks[2], (DEC_BATCH, DEPTH, PAST_LEN, KV_LORA), 1.0),
        'cache_kpe': nrm(ks[3], (DEC_BATCH, DEPTH, PAST_LEN, QK_ROPE), 1.0),
        'c': nrm(ks[4], (DEC_BATCH, D_MODEL), 1.0),
        'c_ctx': nrm(ks[5], (D_MODEL,), 1.0),
        'w_mod': nrm(ks[6], (DEPTH, D_MODEL, N_MOD * D_MODEL), D_MODEL ** -0.5),
        'b_mod': nrm(ks[7], (DEPTH, N_MOD * D_MODEL), 0.01),
        'norm1_w': gain(ks[8], (DEPTH, D_MODEL)),
        'norm2_w': gain(ks[9], (DEPTH, D_MODEL)),
        'w_in': nrm(ks[10], (DEPTH, D_MODEL, IN_COLS), D_MODEL ** -0.5),
        'hy_conv_w': nrm(ks[11], (DEPTH, SHORT_CONV, 3 * D_HYENA), SHORT_CONV ** -0.5),
        'hy_conv_b': nrm(ks[12], (DEPTH, 3 * D_HYENA), 0.01),
        'filt_w1': nrm(ks[13], (DEPTH, FILTER_EMB, FILTER_HIDDEN), FILTER_EMB ** -0.5),
        'filt_b1': nrm(ks[14], (DEPTH, FILTER_HIDDEN), 0.1),
        'filt_w2': nrm(ks[15], (DEPTH, FILTER_HIDDEN, FILTER_HIDDEN), FILTER_HIDDEN ** -0.5),
        'filt_b2': nrm(ks[16], (DEPTH, FILTER_HIDDEN), 0.1),
        'filt_w3': nrm(ks[17], (DEPTH, FILTER_HIDDEN, FILTER_OUT), 0.005),
        'filt_b3': nrm(ks[18], (DEPTH, FILTER_OUT), 0.002),
        'filt_freq': 1.0 + 0.1 * jax.random.normal(ks[19], (DEPTH, 2, FILTER_HIDDEN), f32),
        'hy_skip': 1.0 + 0.1 * jax.random.normal(ks[20], (DEPTH, HYENA_ORDER, D_HYENA), f32),
        'q_a_norm_w': gain(ks[21], (DEPTH, Q_LORA)),
        'w_uq': nrm(ks[22], (DEPTH, Q_LORA, N_HEADS * QK_HEAD), Q_LORA ** -0.5),
        'kv_a_norm_w': gain(ks[23], (DEPTH, KV_LORA)),
        'w_ukv': nrm(ks[24], (DEPTH, KV_LORA, N_HEADS * (QK_NOPE + V_HEAD)), KV_LORA ** -0.5),
        'qn_norm_w': gain(ks[25], (DEPTH, QK_NOPE)),
        'kn_norm_w': gain(ks[26], (DEPTH, QK_NOPE)),
        'qr_norm_w': gain(ks[27], (DEPTH, QK_ROPE)),
        'kr_norm_w': gain(ks[28], (DEPTH, QK_ROPE)),
        'w_hy_out': nrm(ks[29], (DEPTH, D_HYENA, D_MODEL), D_HYENA ** -0.5),
        'w_mla_out': nrm(ks[30], (DEPTH, N_HEADS * V_HEAD, D_MODEL), (N_HEADS * V_HEAD) ** -0.5),
        'w_o': nrm(ks[31], (DEPTH, D_MODEL, D_MODEL), D_MODEL ** -0.5),
        'w_router': nrm(ks[32], (DEPTH, D_MODEL, N_EXPERTS), D_MODEL ** -0.5),
        'b_router': nrm(ks[33], (DEPTH, N_EXPERTS), 0.01),
        'w_gate_up': nrm(ks[34], (DEPTH, N_EXPERTS, D_MODEL, 2 * D_EXPERT), D_MODEL ** -0.5),
        'b_gate_up': nrm(ks[35], (DEPTH, N_EXPERTS, 2 * D_EXPERT), 0.01),
        'w_down': nrm(ks[36], (DEPTH, N_EXPERTS, D_EXPERT, D_MODEL), D_EXPERT ** -0.5),
        'b_down': nrm(ks[37], (DEPTH, N_EXPERTS, D_MODEL), 0.01),
    }


def reference(x_prompt, x_sample, cache_ckv, cache_kpe, c, c_ctx, w_mod, b_mod, norm1_w, norm2_w,
              w_in, hy_conv_w, hy_conv_b, filt_w1, filt_b1, filt_w2, filt_b2, filt_w3, filt_b3,
              filt_freq, hy_skip, q_a_norm_w, w_uq, kv_a_norm_w, w_ukv, qn_norm_w, kn_norm_w,
              qr_norm_w, kr_norm_w, w_hy_out, w_mla_out, w_o, w_router, b_router, w_gate_up,
              b_gate_up, w_down, b_down):
    rope_lat = axial_rope(x_sample.shape[1])
    y_p = x_prompt
    y_s = x_sample
    ckv_layers = []
    kpe_layers = []
    for l in range(DEPTH):
        lp = {
            'norm1': norm1_w[l], 'norm2': norm2_w[l], 'w_in': w_in[l],
            'hy_conv_w': hy_conv_w[l], 'hy_conv_b': hy_conv_b[l],
            'filt_w1': filt_w1[l], 'filt_b1': filt_b1[l], 'filt_w2': filt_w2[l], 'filt_b2': filt_b2[l],
            'filt_w3': filt_w3[l], 'filt_b3': filt_b3[l], 'filt_freq': filt_freq[l], 'hy_skip': hy_skip[l],
            'q_a_norm': q_a_norm_w[l], 'w_uq': w_uq[l], 'kv_a_norm': kv_a_norm_w[l], 'w_ukv': w_ukv[l],
            'qn_norm': qn_norm_w[l], 'kn_norm': kn_norm_w[l], 'qr_norm': qr_norm_w[l], 'kr_norm': kr_norm_w[l],
            'w_hy_out': w_hy_out[l], 'w_mla_out': w_mla_out[l], 'w_o': w_o[l],
            'w_router': w_router[l], 'b_router': b_router[l], 'w_gate_up': w_gate_up[l],
            'b_gate_up': b_gate_up[l], 'w_down': w_down[l], 'b_down': b_down[l],
        }
        mod_ctx = modulation(c_ctx[None, :], w_mod[l], b_mod[l])
        y_p, ckv_l, kpe_l = trunk_layer(y_p, mod_ctx, lp, None, None)
        ckv_layers.append(ckv_l)
        kpe_layers.append(kpe_l)
        mod_lat = modulation(c, w_mod[l], b_mod[l])
        y_s, _, _ = trunk_layer(y_s, mod_lat, lp, rope_lat, (cache_ckv[:, l], cache_kpe[:, l]))
    new_ckv = jnp.stack(ckv_layers, axis=1)
    new_kpe = jnp.stack(kpe_layers, axis=1)
    return (y_p, y_s, new_ckv, new_kpe)
```

```python
import functools
import math

import numpy as np
import jax
import jax.numpy as jnp
from jax import lax
from jax.experimental import pallas as pl
from jax.experimental.pallas import tpu as pltpu

F32 = jnp.float32
BF16 = jnp.bfloat16
I32 = jnp.int32

D_MODEL = 2048
BATCH = 16
SEQ = 256
DEC_BATCH = 2
DEC_SEQ = 4096
PAST_LEN = 256
GRID_W = 64
EPS = 1e-6
N_MOD = 6

D_HYENA = 1024
SHORT_CONV = 3
FILTER_BANDS = 16
FILTER_HIDDEN = 64
DECAY_TARGET = 1e-2
SHORT_DECAY_PCT = 0.3
LONG_DECAY_PCT = 1.5

N_HEADS = 16
QK_NOPE = 128
QK_ROPE = 64
QK_HEAD = QK_NOPE + QK_ROPE
V_HEAD = 128
Q_LORA = 512
KV_LORA = 256
ROPE_THETA = 10000.0

N_EXPERTS = 32
TOP_K = 4
D_EXPERT = 2048
SWIGLU_LIMIT = 7.0
SWIGLU_ALPHA = 1.702

T_CTX = BATCH * SEQ
T_LAT = DEC_BATCH * DEC_SEQ
T_ALL = T_CTX + T_LAT
GROUP_ROWS = 4096
N_GROUPS = T_ALL // GROUP_ROWS
LK_LAT = DEC_SEQ + PAST_LEN

LANE = 128
LAT_COLS = 896

DFT_N1 = 256
PLANE_ROWS = DFT_N1 // 2

MOE_SUB = 256
MOE_SUPER_SUBS = 4
MOE_SUPER = MOE_SUB * MOE_SUPER_SUBS
MOE_CHUNK = 256
N_ASSIGN = T_ALL * TOP_K
MOE_CAP = N_ASSIGN + N_EXPERTS * MOE_SUB
MOE_NSB = (N_ASSIGN // MOE_SUB + N_EXPERTS) // MOE_SUPER_SUBS + N_EXPERTS
ROUTE_TILE = 512

VMEM_LIMIT = 56 * 1024 * 1024


def _params(sem, vmem=None):
    return pltpu.CompilerParams(dimension_semantics=sem, vmem_limit_bytes=vmem or VMEM_LIMIT)


def _rms(x, n=None):
    ms = jnp.sum(x * x, axis=-1, keepdims=True) * (1.0 / (n or x.shape[-1]))
    return x * lax.rsqrt(ms + EPS)


def _dot_nt(a, b):
    return lax.dot_general(a, b, (((1,), (1,)), ((), ())), preferred_element_type=F32)


def _dot_hi(a, b):
    return jnp.dot(a, b, preferred_element_type=F32, precision=lax.Precision.HIGHEST)


def _mod_kernel(c_ref, w_ref, b_ref, o_ref):
    c = c_ref[...]
    a = (c * jax.nn.sigmoid(c)).astype(BF16)
    o_ref[...] = jnp.dot(a, w_ref[...].astype(BF16), preferred_element_type=F32) + b_ref[...]


def _modulation(cvec8, w_mod, b_mod):
    tn = 1024
    n = w_mod.shape[1]
    return pl.pallas_call(
        _mod_kernel,
        grid=(n // tn,),
        in_specs=[pl.BlockSpec((8, D_MODEL), lambda j: (0, 0)),
                  pl.BlockSpec((D_MODEL, tn), lambda j: (0, j)),
                  pl.BlockSpec((1, tn), lambda j: (0, j))],
        out_specs=pl.BlockSpec((8, tn), lambda j: (0, j)),
        out_shape=jax.ShapeDtypeStruct((8, n), F32),
        compiler_params=_params(("arbitrary",)),
        name="modulation",
    )(cvec8, w_mod, b_mod.reshape(1, n))


def _prenorm_kernel(x_ref, nw_ref, sh_ref, sc_ref, o_ref):
    y = _rms(x_ref[...]) * nw_ref[...]
    o_ref[...] = (y * (1.0 + sc_ref[0]) + sh_ref[0]).astype(o_ref.dtype)


def _prenorm(x, norm_w, mod3, shift_idx, scale_idx):
    tm = 512
    per = GROUP_ROWS // tm
    return pl.pallas_call(
        _prenorm_kernel,
        grid=(T_ALL // tm,),
        in_specs=[pl.BlockSpec((tm, D_MODEL), lambda i: (i, 0)),
                  pl.BlockSpec((1, D_MODEL), lambda i: (0, 0)),
                  pl.BlockSpec((1, 1, D_MODEL), lambda i: ((i // per) * N_MOD + shift_idx, 0, 0)),
                  pl.BlockSpec((1, 1, D_MODEL), lambda i: ((i // per) * N_MOD + scale_idx, 0, 0))],
        out_specs=pl.BlockSpec((tm, D_MODEL), lambda i: (i, 0)),
        out_shape=jax.ShapeDtypeStruct((T_ALL, D_MODEL), BF16),
        compiler_params=_params(("arbitrary",)),
        name="prenorm",
    )(x, norm_w.reshape(1, D_MODEL), mod3, mod3)


def _mm_kernel(a_ref, b_ref, o_ref, *, act):
    r = jnp.dot(a_ref[...], b_ref[...], preferred_element_type=F32)
    if act == "sigmoid":
        r = jax.nn.sigmoid(r)
    o_ref[...] = r.astype(o_ref.dtype)


def _matmul(a, b, *, tm, tn, out_dtype, act=None, name="matmul"):
    m, k = a.shape
    n = b.shape[1]
    return pl.pallas_call(
        functools.partial(_mm_kernel, act=act),
        grid=(n // tn, m // tm),
        in_specs=[pl.BlockSpec((tm, k), lambda j, i: (i, 0)),
                  pl.BlockSpec((k, tn), lambda j, i: (0, j))],
        out_specs=pl.BlockSpec((tm, tn), lambda j, i: (i, j)),
        out_shape=jax.ShapeDtypeStruct((m, n), out_dtype),
        compiler_params=_params(("arbitrary", "arbitrary")),
        name=name,
    )(a, b)


def _rope128(y, cos, sin, lane):
    rot = jnp.where(lane < QK_ROPE // 2, -pltpu.roll(y, LANE - QK_ROPE // 2, 1), pltpu.roll(y, QK_ROPE // 2, 1))
    return y * cos + rot * sin


def _qkv_kernel(lat_ref, qa_ref, wuq_ref, qn_ref, qr_ref, kva_ref, kr_ref, cos_ref, sin_ref,
                q_ref, ckv_ref, kpe_ref):
    lat = lat_ref[...]
    cos = cos_ref[...]
    sin = sin_ref[...]
    lane = lax.broadcasted_iota(I32, cos.shape, 1)
    qc = (_rms(lat[:, :Q_LORA]) * qa_ref[...]).astype(BF16)
    q = jnp.dot(qc, wuq_ref[...], preferred_element_type=F32)
    scale = QK_HEAD ** -0.5
    nope_cols = N_HEADS * QK_NOPE
    for h in range(N_HEADS):
        sl = slice(h * LANE, (h + 1) * LANE)
        q_ref[:, sl] = (_rms(q[:, sl]) * qn_ref[:, sl] * scale).astype(BF16)
        sr = slice(nope_cols + h * LANE, nope_cols + (h + 1) * LANE)
        yr = _rms(q[:, sr], QK_ROPE) * qr_ref[...]
        q_ref[:, sr] = (_rope128(yr, cos, sin, lane) * scale).astype(BF16)
    ckv_ref[...] = _rms(lat[:, Q_LORA:Q_LORA + KV_LORA]) * kva_ref[...]
    kp = _rms(lat[:, Q_LORA + KV_LORA:], QK_ROPE) * kr_ref[...]
    kpe_ref[...] = _rope128(kp, cos, sin, lane)


def _qkv(lat, qa_w, wuq_p, qn_w, qr_w, kva_w, kr_w, cos_t, sin_t):
    tm = 256
    qcols = 2 * N_HEADS * LANE
    row = lambda i: (i, 0)
    fix = lambda i: (0, 0)
    return pl.pallas_call(
        _qkv_kernel,
        grid=(T_ALL // tm,),
        in_specs=[pl.BlockSpec((tm, LAT_COLS), row),
                  pl.BlockSpec((1, Q_LORA), fix),
                  pl.BlockSpec((Q_LORA, qcols), fix),
                  pl.BlockSpec((1, N_HEADS * LANE), fix),
                  pl.BlockSpec((1, LANE), fix),
                  pl.BlockSpec((1, KV_LORA), fix),
                  pl.BlockSpec((1, LANE), fix),
                  pl.BlockSpec((tm, LANE), row),
                  pl.BlockSpec((tm, LANE), row)],
        out_specs=[pl.BlockSpec((tm, qcols), row),
                   pl.BlockSpec((tm, KV_LORA), row),
                   pl.BlockSpec((tm, LANE), row)],
        out_shape=[jax.ShapeDtypeStruct((T_ALL, qcols), BF16),
                   jax.ShapeDtypeStruct((T_ALL, KV_LORA), F32),
                   jax.ShapeDtypeStruct((T_ALL, LANE), F32)],
        compiler_params=_params(("arbitrary",)),
        name="mla_qkv",
    )(lat, qa_w, wuq_p, qn_w, qr_w, kva_w, kr_w, cos_t, sin_t)


def _kvup_kernel(c_ref, w_ref, kn_ref, k_ref, v_ref):
    kv = jnp.dot(c_ref[...].astype(BF16), w_ref[...], preferred_element_type=F32)
    half = N_HEADS * LANE
    for h in range(N_HEADS):
        sl = slice(h * LANE, (h + 1) * LANE)
        k_ref[:, sl] = (_rms(kv[:, sl]) * kn_ref[:, sl]).astype(BF16)
    v_ref[...] = kv[:, half:].astype(BF16)


def _kvup(ckv_rows, wukv_p, kn_w):
    tm = 256
    rows = ckv_rows.shape[0]
    half = N_HEADS * LANE
    return pl.pallas_call(
        _kvup_kernel,
        grid=(rows // tm,),
        in_specs=[pl.BlockSpec((tm, KV_LORA), lambda i: (i, 0)),
                  pl.BlockSpec((KV_LORA, 2 * half), lambda i: (0, 0)),
                  pl.BlockSpec((1, half), lambda i: (0, 0))],
        out_specs=[pl.BlockSpec((tm, half), lambda i: (i, 0)),
                   pl.BlockSpec((tm, half), lambda i: (i, 0))],
        out_shape=[jax.ShapeDtypeStruct((rows, half), BF16),
                   jax.ShapeDtypeStruct((rows, half), BF16)],
        compiler_params=_params(("arbitrary",)),
        name="mla_kv_up",
    )(ckv_rows, wukv_p, kn_w)


def _attend(qn, qp, kn, kp, v):
    s = _dot_nt(qn, kn) + _dot_nt(qp, kp)
    m = jnp.max(s, axis=-1, keepdims=True)
    p = jnp.exp(s - m)
    l = jnp.sum(p, axis=-1, keepdims=True)
    o = jnp.dot(p.astype(BF16), v, preferred_element_type=F32)
    return o / l


def _attn_ctx_kernel(q_ref, k_ref, v_ref, kp_ref, o_ref):
    kp = kp_ref[...]
    nope_cols = N_HEADS * LANE
    for h in range(N_HEADS):
        sl = slice(h * LANE, (h + 1) * LANE)
        sr = slice(nope_cols + h * LANE, nope_cols + (h + 1) * LANE)
        o_ref[:, sl] = _attend(q_ref[:, sl], q_ref[:, sr], k_ref[:, sl], kp, v_ref[:, sl]).astype(BF16)


def _attn_ctx(q_all, k_ctx, v_ctx, kpe_ctx):
    half = N_HEADS * LANE
    row = lambda b: (b, 0)
    return pl.pallas_call(
        _attn_ctx_kernel,
        grid=(BATCH,),
        in_specs=[pl.BlockSpec((SEQ, 2 * half), row),
                  pl.BlockSpec((SEQ, half), row),
                  pl.BlockSpec((SEQ, half), row),
                  pl.BlockSpec((SEQ, LANE), row)],
        out_specs=pl.BlockSpec((SEQ, half), row),
        out_shape=jax.ShapeDtypeStruct((T_CTX, half), BF16),
        compiler_params=_params(("arbitrary",)),
        name="attn_ctx",
    )(q_all, k_ctx, v_ctx, kpe_ctx)


def _attn_lat_kernel(qn_ref, qp_ref, k_ref, v_ref, kp_ref, o_ref):
    o_ref[...] = _attend(qn_ref[...], qp_ref[...], k_ref[...], kp_ref[...], v_ref[...]).astype(BF16)


def _attn_lat(q_all, k_lat, v_lat, kpe_lat):
    tq = 512
    nq = DEC_SEQ // tq
    off = T_CTX // tq
    return pl.pallas_call(
        _attn_lat_kernel,
        grid=(DEC_BATCH, N_HEADS, nq),
        in_specs=[pl.BlockSpec((tq, LANE), lambda b, h, i: (off + b * nq + i, h)),
                  pl.BlockSpec((tq, LANE), lambda b, h, i: (off + b * nq + i, N_HEADS + h)),
                  pl.BlockSpec((LK_LAT, LANE), lambda b, h, i: (b, h)),
                  pl.BlockSpec((LK_LAT, LANE), lambda b, h, i: (b, h)),
                  pl.BlockSpec((LK_LAT, LANE), lambda b, h, i: (b, 0))],
        out_specs=pl.BlockSpec((tq, LANE), lambda b, h, i: (b * nq + i, h)),
        out_shape=jax.ShapeDtypeStruct((T_LAT, N_HEADS * V_HEAD), BF16),
        compiler_params=_params(("arbitrary", "arbitrary", "arbitrary")),
        name="attn_lat",
    )(q_all, q_all, k_lat, v_lat, kpe_lat)


def _dft_constants(n2):
    n = n2 * DFT_N1
    a = np.arange(PLANE_ROWS, dtype=np.float64)[None, None, :]
    p = np.arange(n2, dtype=np.float64)[:, None, None]
    k1 = np.arange(DFT_N1, dtype=np.float64)[None, :, None]
    theta = 2.0 * np.pi * ((n2 * a + p) * k1 % n) / n
    fwd = np.concatenate([np.cos(theta), -np.sin(theta)], axis=1)
    return fwd, np.transpose(fwd, (0, 2, 1))


def _radix_forward(y_ref, n2):
    m = n2 // 2
    while m >= 1:
        for j in range(m):
            ang = -2.0 * math.pi * j / (2 * m)
            wr, wi = math.cos(ang), math.sin(ang)

            def body(g, c, j=j, m=m, wr=wr, wi=wi):
                p = g * (2 * m) + j
                q = p + m
                a = y_ref[p]
                b = y_ref[q]
                y_ref[p] = a + b
                d = a - b
                if j == 0:
                    y_ref[q] = d
                else:
                    dr, di = d[:DFT_N1], d[DFT_N1:]
                    y_ref[q, :DFT_N1] = dr * wr - di * wi
                    y_ref[q, DFT_N1:] = dr * wi + di * wr
                return c

            lax.fori_loop(0, n2 // (2 * m), body, 0)
        m //= 2


def _radix_inverse(y_ref, n2):
    m = 1
    while m <= n2 // 2:
        for j in range(m):
            ang = 2.0 * math.pi * j / (2 * m)
            wr, wi = math.cos(ang), math.sin(ang)

            def body(g, c, j=j, m=m, wr=wr, wi=wi):
                p = g * (2 * m) + j
                q = p + m
                a = y_ref[p]
                b = y_ref[q]
                if j != 0:
                    br, bi = b[:DFT_N1], b[DFT_N1:]
                    b = jnp.concatenate([br * wr - bi * wi, br * wi + bi * wr], axis=0)
                y_ref[p] = a + b
                y_ref[q] = a - b
                return c

            lax.fori_loop(0, n2 // (2 * m), body, 0)
        m *= 2


def _filt_hidden_kernel(bands_ref, w1t_ref, w1c_ref, w1s_ref, b1_ref, w2_ref, b2_ref, fr_ref, o_ref, *, n):
    rows = o_ref.shape[0]
    pos = (lax.broadcasted_iota(I32, (rows, 1), 0) - 8).astype(F32)
    t = pos / (n - 1.0)
    bw = bands_ref[...] * ((2.0 * math.pi / n) * pos)
    z = t * w1t_ref[...] + _dot_hi(jnp.cos(bw), w1c_ref[...]) + _dot_hi(-jnp.sin(bw), w1s_ref[...]) + b1_ref[...]
    h = jnp.sin(fr_ref[0:1, :FILTER_HIDDEN] * z)
    o_ref[...] = jnp.sin(fr_ref[1:2, :] * (_dot_hi(h, w2_ref[...]) + b2_ref[...]))


def _filt_hidden(n, bands, w1, b1, w2, b2, freq):
    nb = FILTER_BANDS
    return pl.pallas_call(
        functools.partial(_filt_hidden_kernel, n=n),
        out_shape=jax.ShapeDtypeStruct((n + 8, LANE), F32),
        compiler_params=_params(None),
        name=f"filt_hidden_{n}",
    )(bands, w1[0:1], w1[1:1 + nb], w1[1 + nb:], b1.reshape(1, -1), _pad_lanes(w2, LANE),
      _pad_lanes(b2.reshape(1, -1), LANE), _pad_lanes(freq, LANE))


def _filt_spec_kernel(hdn_ref, w3f_ref, b3f_ref, w3b_ref, b3b_ref, dl_ref, fh_ref, fl_ref, o_ref, yb_ref, *, n, n2):
    inv_n = 1.0 / (2 * n)
    absd = jnp.abs(dl_ref[...])

    def plane(p, c):
        a = lax.broadcasted_iota(I32, (PLANE_ROWS, 1), 0)
        fh = fh_ref[p]
        fl = fl_ref[p]

        def spectrum(shift, w3_ref, b3_ref):
            pos = a * n2 + p - shift
            rows = hdn_ref[pl.ds(8 + p - shift, PLANE_ROWS, stride=n2), :]
            t = pos.astype(F32) / (n - 1.0)
            h = (_dot_hi(rows, w3_ref[...]) + b3_ref[...]) * jnp.exp(-t * absd)
            h = jnp.where(pos >= 0, h, 0.0)
            hh = h.astype(BF16)
            hl = (h - hh.astype(F32)).astype(BF16)
            return (jnp.dot(fh, hh, preferred_element_type=F32) + jnp.dot(fl, hh, preferred_element_type=F32)
                    + jnp.dot(fh, hl, preferred_element_type=F32))

        o_ref[0, p] = spectrum(0, w3f_ref, b3f_ref)
        yb_ref[p] = spectrum(1, w3b_ref, b3b_ref)
        return c

    lax.fori_loop(0, n2, plane, 0)
    _radix_forward(o_ref.at[0], n2)
    _radix_forward(yb_ref, n2)

    def combine(p, c):
        f = o_ref[0, p]
        b = yb_ref[p]
        o_ref[0, p, :DFT_N1] = (f[:DFT_N1] + b[:DFT_N1]) * inv_n
        o_ref[0, p, DFT_N1:] = (f[DFT_N1:] - b[DFT_N1:]) * inv_n
        return c

    lax.fori_loop(0, n2, combine, 0)


def _filt_spectrum(n, n2, hdn, w3, b3, deltas, f_hi, f_lo):
    tc = LANE
    nct = D_HYENA // tc
    w3p = jnp.concatenate([w3, jnp.zeros((LANE - FILTER_HIDDEN, w3.shape[1]), F32)], axis=0)
    return pl.pallas_call(
        functools.partial(_filt_spec_kernel, n=n, n2=n2),
        grid=(2, nct),
        in_specs=[pl.BlockSpec((n + 8, LANE), lambda o, c: (0, 0)),
                  pl.BlockSpec((LANE, tc), lambda o, c: (0, o * 2 * nct + c)),
                  pl.BlockSpec((1, tc), lambda o, c: (0, o * 2 * nct + c)),
                  pl.BlockSpec((LANE, tc), lambda o, c: (0, o * 2 * nct + nct + c)),
                  pl.BlockSpec((1, tc), lambda o, c: (0, o * 2 * nct + nct + c)),
                  pl.BlockSpec((1, tc), lambda o, c: (0, c)),
                  pl.BlockSpec((n2, 2 * DFT_N1, PLANE_ROWS), lambda o, c: (0, 0, 0)),
                  pl.BlockSpec((n2, 2 * DFT_N1, PLANE_ROWS), lambda o, c: (0, 0, 0))],
        out_specs=pl.BlockSpec((1, n2, 2 * DFT_N1, tc), lambda o, c: (o, 0, 0, c)),
        out_shape=jax.ShapeDtypeStruct((2, n2, 2 * DFT_N1, D_HYENA), F32),
        scratch_shapes=[pltpu.VMEM((n2, 2 * DFT_N1, tc), F32)],
        compiler_params=_params(("arbitrary", "arbitrary")),
        name=f"filt_spectrum_{n}",
    )(hdn, w3p, b3.reshape(1, -1), w3p, b3.reshape(1, -1), deltas, f_hi, f_lo)


def _hyena_kernel(*refs, n, n2, a_planes, out_dtype):
    if a_planes:
        a_ref, g_ref, gw_ref, gb_ref, kf_ref, sk_ref, f_ref, fi_ref, o_ref, y_ref, gpad = refs
    else:
        a_ref, aw_ref, ab_ref, g_ref, gw_ref, gb_ref, kf_ref, sk_ref, f_ref, fi_ref, o_ref, y_ref, gpad, apad = refs
    tc = o_ref.shape[-1]
    zeros8 = jnp.zeros((8, tc), F32)

    def stage(pad_ref, src_ref):
        pad_ref[0:8, :] = zeros8
        pad_ref[n + 8:n + 16, :] = zeros8
        pad_ref[8:n + 8, :] = src_ref[0]

    def conv_plane(pad_ref, w_ref, b_ref, p):
        out = b_ref[...]
        for j in range(SHORT_CONV):
            out = out + pad_ref[pl.ds(7 + j + p, PLANE_ROWS, stride=n2), :] * w_ref[j:j + 1, :]
        return out

    stage(gpad, g_ref)
    if not a_planes:
        stage(apad, a_ref)

    def a_plane(p):
        return a_ref[0, p] if a_planes else conv_plane(apad, aw_ref, ab_ref, p)

    def fwd(p, c):
        y_ref[p] = jnp.dot(f_ref[p], a_plane(p).astype(BF16), preferred_element_type=F32)
        return c

    lax.fori_loop(0, n2, fwd, 0)
    _radix_forward(y_ref, n2)

    def mul(p, c):
        y = y_ref[p]
        k = kf_ref[0, p]
        yr, yi = y[:DFT_N1], y[DFT_N1:]
        kr, ki = k[:DFT_N1], k[DFT_N1:]
        y_ref[p, :DFT_N1] = yr * kr - yi * ki
        y_ref[p, DFT_N1:] = yr * ki + yi * kr
        return c

    lax.fori_loop(0, n2, mul, 0)
    _radix_inverse(y_ref, n2)

    def inv(p, c):
        conv = jnp.dot(fi_ref[p], y_ref[p].astype(BF16), preferred_element_type=F32)
        gate = conv_plane(gpad, gw_ref, gb_ref, p)
        o_ref[0, p] = (gate * (conv + sk_ref[0] * a_plane(p))).astype(out_dtype)
        return c

    lax.fori_loop(0, n2, inv, 0)


def _hyena_order(u3, a_src, a_col, g_col, conv_w, conv_b, kf, order, skip, f_fwd, f_inv, *, n, n2, tc, out_dtype):
    nb = u3.shape[0]
    nct = D_HYENA // tc
    a_planes = a_src is not None
    tok = lambda col: pl.BlockSpec((1, n, tc), lambda c, b: (b, 0, col * nct + c))
    cw = lambda col: pl.BlockSpec((SHORT_CONV, tc), lambda c, b: (0, col * nct + c))
    cb = lambda col: pl.BlockSpec((1, tc), lambda c, b: (0, col * nct + c))
    planes = pl.BlockSpec((1, n2, PLANE_ROWS, tc), lambda c, b: (b, 0, 0, c))
    if a_planes:
        in_specs = [planes]
        args = [a_src]
    else:
        in_specs = [tok(a_col), cw(a_col), cb(a_col)]
        args = [u3, conv_w, conv_b]
    in_specs += [tok(g_col), cw(g_col), cb(g_col),
                 pl.BlockSpec((1, n2, 2 * DFT_N1, tc), lambda c, b: (order, 0, 0, c), pipeline_mode=pl.Buffered(1)),
                 pl.BlockSpec((1, 1, tc), lambda c, b: (order, 0, c)),
                 pl.BlockSpec((n2, 2 * DFT_N1, PLANE_ROWS), lambda c, b: (0, 0, 0), pipeline_mode=pl.Buffered(1)),
                 pl.BlockSpec((n2, PLANE_ROWS, 2 * DFT_N1), lambda c, b: (0, 0, 0), pipeline_mode=pl.Buffered(1))]
    args += [u3, conv_w, conv_b, kf, skip, f_fwd, f_inv]
    scratch = [pltpu.VMEM((n2, 2 * DFT_N1, tc), F32), pltpu.VMEM((n + 16, tc), F32)]
    if not a_planes:
        scratch.append(pltpu.VMEM((n + 16, tc), F32))
    return pl.pallas_call(
        functools.partial(_hyena_kernel, n=n, n2=n2, a_planes=a_planes, out_dtype=out_dtype),
        grid=(nct, nb),
        in_specs=in_specs,
        out_specs=planes,
        out_shape=jax.ShapeDtypeStruct((nb, n2, PLANE_ROWS, D_HYENA), out_dtype),
        scratch_shapes=scratch,
        compiler_params=_params(("arbitrary", "arbitrary")),
        name=f"hyena_{n}_{'b' if a_planes else 'a'}",
    )(*args)


def _hyena_group(u3, n, tc, lp):
    n2 = 2 * n // DFT_N1
    fwd, inv = _dft_constants(n2)
    f_hi = jnp.asarray(fwd, F32).astype(BF16)
    f_lo = (jnp.asarray(fwd, F32) - f_hi.astype(F32)).astype(BF16)
    f_inv = jnp.asarray(inv, F32).astype(BF16)
    hdn = _filt_hidden(n, lp["bands"], lp["filt_w1"], lp["filt_b1"], lp["filt_w2"], lp["filt_b2"], lp["filt_freq"])
    kf = _filt_spectrum(n, n2, hdn, lp["filt_w3"], lp["filt_b3"], lp["deltas"], f_hi, f_lo)
    common = dict(n=n, n2=n2, tc=tc)
    z = _hyena_order(u3, None, 0, 1, lp["hy_conv_w"], lp["hy_conv_b"], kf, 0, lp["hy_skip"], f_hi, f_inv,
                     out_dtype=F32, **common)
    y = _hyena_order(u3, z, None, 2, lp["hy_conv_w"], lp["hy_conv_b"], kf, 1, lp["hy_skip"], f_hi, f_inv,
                     out_dtype=BF16, **common)
    nb = u3.shape[0]
    return jnp.transpose(y, (0, 2, 1, 3)).reshape(nb * n, D_HYENA)


def _merge_kernel(yhc_ref, yhl_ref, ymc_ref, yml_ref, g_ref, wh_ref, wm_ref, o_ref, *, ctx_tiles):
    def run(yh_ref, ym_ref):
        a = jnp.dot(yh_ref[...], wh_ref[...], preferred_element_type=F32)
        b = jnp.dot(ym_ref[...], wm_ref[...], preferred_element_type=F32)
        o_ref[...] = (g_ref[:, :D_MODEL] * a + g_ref[:, D_MODEL:] * b).astype(BF16)

    is_ctx = pl.program_id(0) < ctx_tiles
    pl.when(is_ctx)(lambda: run(yhc_ref, ymc_ref))
    pl.when(jnp.logical_not(is_ctx))(lambda: run(yhl_ref, yml_ref))


def _merge(yh_ctx, yh_lat, ym_ctx, ym_lat, gates, w_hy, w_mla):
    tm = 256
    nc = T_CTX // tm
    row = lambda i: (i, 0)
    fix = lambda i: (0, 0)
    ctx = lambda i: (jnp.minimum(i, nc - 1), 0)
    lat = lambda i: (jnp.maximum(i - nc, 0), 0)
    return pl.pallas_call(
        functools.partial(_merge_kernel, ctx_tiles=nc),
        grid=(T_ALL // tm,),
        in_specs=[pl.BlockSpec((tm, D_HYENA), ctx),
                  pl.BlockSpec((tm, D_HYENA), lat),
                  pl.BlockSpec((tm, N_HEADS * V_HEAD), ctx),
                  pl.BlockSpec((tm, N_HEADS * V_HEAD), lat),
                  pl.BlockSpec((tm, 2 * D_MODEL), row),
                  pl.BlockSpec((D_HYENA, D_MODEL), fix, pipeline_mode=pl.Buffered(1)),
                  pl.BlockSpec((N_HEADS * V_HEAD, D_MODEL), fix, pipeline_mode=pl.Buffered(1))],
        out_specs=pl.BlockSpec((tm, D_MODEL), row),
        out_shape=jax.ShapeDtypeStruct((T_ALL, D_MODEL), BF16),
        compiler_params=_params(("arbitrary",)),
        name="merge",
    )(yh_ctx, yh_lat, ym_ctx, ym_lat, gates, w_hy, w_mla)


def _oproj_kernel(m_ref, x_ref, wo_ref, g1_ref, nw_ref, sh_ref, sc_ref, wrh_ref, wrl_ref, br_ref,
                  x1_ref, h2_ref, lg_ref):
    mix = jnp.dot(m_ref[...], wo_ref[...], preferred_element_type=F32)
    x1 = x_ref[...] + g1_ref[0] * mix
    x1_ref[...] = x1
    h2 = _rms(x1) * nw_ref[...] * (1.0 + sc_ref[0]) + sh_ref[0]
    h2_ref[...] = h2
    hh = h2.astype(BF16)
    hl = (h2 - hh.astype(F32)).astype(BF16)
    lg_ref[...] = (_dot_nt(wrh_ref[...], hh) + _dot_nt(wrh_ref[...], hl) + _dot_nt(wrl_ref[...], hh)
                   + br_ref[:, 0:1])


def _oproj(merged, x, w_o, mod3, norm2_w, wr_hi, wr_lo, b_router):
    tm = 256
    per = GROUP_ROWS // tm
    row = lambda i: (i, 0)
    fix = lambda i: (0, 0)
    modv = lambda k: pl.BlockSpec((1, 1, D_MODEL), lambda i: ((i // per) * N_MOD + k, 0, 0))
    return pl.pallas_call(
        _oproj_kernel,
        grid=(T_ALL // tm,),
        in_specs=[pl.BlockSpec((tm, D_MODEL), row),
                  pl.BlockSpec((tm, D_MODEL), row),
                  pl.BlockSpec((D_MODEL, D_MODEL), fix, pipeline_mode=pl.Buffered(1)),
                  modv(2),
                  pl.BlockSpec((1, D_MODEL), fix),
                  modv(3), modv(4),
                  pl.BlockSpec((N_EXPERTS, D_MODEL), fix),
                  pl.BlockSpec((N_EXPERTS, D_MODEL), fix),
                  pl.BlockSpec((N_EXPERTS, LANE), fix)],
        out_specs=[pl.BlockSpec((tm, D_MODEL), row),
                   pl.BlockSpec((tm, D_MODEL), row),
                   pl.BlockSpec((N_EXPERTS, tm), lambda i: (0, i))],
        out_shape=[jax.ShapeDtypeStruct((T_ALL, D_MODEL), F32),
                   jax.ShapeDtypeStruct((T_ALL, D_MODEL), F32),
                   jax.ShapeDtypeStruct((N_EXPERTS, T_ALL), F32)],
        compiler_params=_params(("arbitrary",)),
        name="oproj_norm_router",
    )(merged, x, w_o, mod3, norm2_w.reshape(1, D_MODEL), mod3, mod3, wr_hi, wr_lo, b_router)


def _route_kernel(lg_ref, tri_ref, low_ref, w_ref, dest_ref, cnt_ref, start_ref, idx_scr):
    nt = T_ALL // ROUTE_TILE
    eid = lax.broadcasted_iota(I32, (N_EXPERTS, ROUTE_TILE), 0).astype(F32)

    def tile(i, carry):
        sl = pl.ds(pl.multiple_of(i * ROUTE_TILE, ROUTE_TILE), ROUTE_TILE)
        vals = lg_ref[:, sl]
        mask = jnp.zeros((N_EXPERTS, ROUTE_TILE), F32)
        tops, hots = [], []
        for k in range(TOP_K):
            m = jnp.max(vals, axis=0, keepdims=True)
            sel = jnp.min(jnp.where(vals == m, eid, float(N_EXPERTS)), axis=0, keepdims=True)
            hot = eid == sel
            vals = jnp.where(hot, -jnp.inf, vals)
            mask = mask + hot.astype(F32)
            tops.append(m)
            hots.append(hot)
            idx_scr[k:k + 1, sl] = sel
        ex = [jnp.exp(t - tops[0]) for t in tops]
        den = ex[0] + ex[1] + ex[2] + ex[3]
        before = jnp.dot(mask.astype(BF16), tri_ref[...], preferred_element_type=F32) + carry
        for k in range(TOP_K):
            w_ref[k:k + 1, sl] = ex[k] / den
            rank = jnp.sum(jnp.where(hots[k], before, 0.0), axis=0, keepdims=True)
            dest_ref[k:k + 1, sl] = rank.astype(I32)
        return carry + jnp.sum(mask, axis=1, keepdims=True)

    counts = lax.fori_loop(0, nt, tile, jnp.zeros((N_EXPERTS, 1), F32))
    nsub = jnp.floor((counts + (MOE_SUB - 1.0)) * (1.0 / MOE_SUB))
    nsub_b = jnp.broadcast_to(nsub, (N_EXPERTS, LANE)).astype(BF16)
    start = jnp.dot(low_ref[...], nsub_b, preferred_element_type=F32) * float(MOE_SUB)
    cnt_ref[...] = jnp.broadcast_to(counts, (N_EXPERTS, LANE)).astype(I32)
    start_ref[...] = start.astype(I32)
    start_col = start[:, 0:1]

    def place(i, c):
        sl = pl.ds(pl.multiple_of(i * ROUTE_TILE, ROUTE_TILE), ROUTE_TILE)
        for k in range(TOP_K):
            hot = eid == idx_scr[k:k + 1, sl]
            base = jnp.sum(jnp.where(hot, start_col, 0.0), axis=0, keepdims=True)
            dest_ref[k:k + 1, sl] = dest_ref[k:k + 1, sl] + base.astype(I32)
        return c

    lax.fori_loop(0, nt, place, 0)


def _route(logits_t):
    r = np.arange(ROUTE_TILE)
    tri = jnp.asarray((r[:, None] < r[None, :]).astype(np.float32)).astype(BF16)
    e = np.arange(N_EXPERTS)
    low = jnp.asarray((e[None, :] < e[:, None]).astype(np.float32)).astype(BF16)
    return pl.pallas_call(
        _route_kernel,
        out_shape=[jax.ShapeDtypeStruct((TOP_K, T_ALL), F32),
                   jax.ShapeDtypeStruct((TOP_K, T_ALL), I32),
                   jax.ShapeDtypeStruct((N_EXPERTS, LANE), I32),
                   jax.ShapeDtypeStruct((N_EXPERTS, LANE), I32)],
        scratch_shapes=[pltpu.VMEM((TOP_K, T_ALL), F32)],
        compiler_params=_params(None),
        name="route",
    )(logits_t, tri, low)


def _row_copy(src, src_row, dst, dst_row, sem):
    return pltpu.make_async_copy(src.at[pl.ds(src_row, 1)], dst.at[pl.ds(dst_row, 1)], sem)


def _dispatch_kernel(dest_ref, cnt_ref, start_ref, h_hbm, xs_hbm, zero_ref, sem, zsem, *, tt):
    i = pl.program_id(0)

    @pl.when(i == 0)
    def _():
        zero_ref[...] = jnp.zeros(zero_ref.shape, F32)

        def per_expert(e, c):
            cnt = cnt_ref[e]
            pad = (MOE_SUB - cnt % MOE_SUB) % MOE_SUB
            base = start_ref[e] + cnt

            def issue(r, cc):
                _row_copy(zero_ref, 0, xs_hbm, base + r, zsem).start()
                return cc

            lax.fori_loop(0, pad, issue, 0)

            def drain(r, cc):
                _row_copy(zero_ref, 0, xs_hbm, base + r, zsem).wait()
                return cc

            lax.fori_loop(0, pad, drain, 0)
            return c

        lax.fori_loop(0, N_EXPERTS, per_expert, 0)

        last = N_EXPERTS - 1
        first_unused = (start_ref[last] + cnt_ref[last] + MOE_SUB - 1) // MOE_SUB

        def tail(b):
            return pltpu.make_async_copy(zero_ref, xs_hbm.at[pl.ds(pl.multiple_of(b * MOE_SUB, MOE_SUB), MOE_SUB)], zsem)

        def tail_issue(b, c):
            tail(b).start()
            return c

        def tail_drain(b, c):
            tail(b).wait()
            return c

        lax.fori_loop(first_unused, MOE_CAP // MOE_SUB, tail_issue, 0)
        lax.fori_loop(first_unused, MOE_CAP // MOE_SUB, tail_drain, 0)

    def issue(t, c):
        tok = i * tt + t
        for k in range(TOP_K):
            _row_copy(h_hbm, tok, xs_hbm, dest_ref[k * T_ALL + tok], sem).start()
        return c

    lax.fori_loop(0, tt, issue, 0)

    def drain(t, c):
        tok = i * tt + t
        for k in range(TOP_K):
            _row_copy(h_hbm, tok, xs_hbm, dest_ref[k * T_ALL + tok], sem).wait()
        return c

    lax.fori_loop(0, tt, drain, 0)


def _dispatch(dest_flat, counts, starts, h2):
    tt = 512
    return pl.pallas_call(
        functools.partial(_dispatch_kernel, tt=tt),
        grid_spec=pltpu.PrefetchScalarGridSpec(
            num_scalar_prefetch=3,
            grid=(T_ALL // tt,),
            in_specs=[pl.BlockSpec(memory_space=pl.ANY)],
            out_specs=pl.BlockSpec(memory_space=pl.ANY),
            scratch_shapes=[pltpu.VMEM((MOE_SUB, D_MODEL), F32), pltpu.SemaphoreType.DMA(()),
                            pltpu.SemaphoreType.DMA(())]),
        out_shape=jax.ShapeDtypeStruct((MOE_CAP, D_MODEL), F32),
        compiler_params=_params(("arbitrary",)),
        name="dispatch",
    )(dest_flat, counts, starts, h2)


def _moe_kernel(e_tbl, row_tbl, ns_tbl, tail_tbl, xs_hbm, wgu_ref, bgu_ref, wd_ref, bd_ref, psel_ref, out_hbm,
                x32, xbf, acc, wgu_bf, wd_bf, sem):
    del e_tbl
    sb = pl.program_id(0)
    j = pl.program_id(1)
    nsub = ns_tbl[sb]
    row0 = row_tbl[sb]

    @pl.when(jnp.logical_and(sb == 0, j == 0))
    def _():
        acc[0:MOE_SUB, :] = jnp.zeros((MOE_SUB, D_MODEL), F32)

        def tail(b):
            dst = out_hbm.at[pl.ds(pl.multiple_of(b * MOE_SUB, MOE_SUB), MOE_SUB)]
            return pltpu.make_async_copy(acc.at[pl.ds(0, MOE_SUB)], dst, sem)

        def tail_issue(b, c):
            tail(b).start()
            return c

        def tail_drain(b, c):
            tail(b).wait()
            return c

        lax.fori_loop(tail_tbl[0], MOE_CAP // MOE_SUB, tail_issue, 0)
        lax.fori_loop(tail_tbl[0], MOE_CAP // MOE_SUB, tail_drain, 0)

    def sub_copy(src, dst, i, to_hbm):
        lo = pl.ds(pl.multiple_of(i * MOE_SUB, MOE_SUB), MOE_SUB)
        hi = pl.ds(pl.multiple_of(row0 + i * MOE_SUB, MOE_SUB), MOE_SUB)
        if to_hbm:
            return pltpu.make_async_copy(src.at[lo], dst.at[hi], sem)
        return pltpu.make_async_copy(src.at[hi], dst.at[lo], sem)

    def for_subs(fn):
        def body(i, c):
            fn(i)
            return c
        lax.fori_loop(0, nsub, body, 0)

    @pl.when(nsub > 0)
    def _():
        @pl.when(j == 0)
        def _():
            for_subs(lambda i: sub_copy(xs_hbm, x32, i, False).start())
            for_subs(lambda i: sub_copy(xs_hbm, x32, i, False).wait())

            def init(i):
                lo = pl.ds(pl.multiple_of(i * MOE_SUB, MOE_SUB), MOE_SUB)
                xbf[lo, :] = x32[lo, :].astype(BF16)
                acc[lo, :] = jnp.broadcast_to(bd_ref[0], (MOE_SUB, D_MODEL))

            for_subs(init)

        wgu_bf[...] = wgu_ref[0].astype(BF16)
        wd_bf[...] = wd_ref[0].astype(BF16)
        bgu = bgu_ref[0]

        def compute(i):
            lo = pl.ds(pl.multiple_of(i * MOE_SUB, MOE_SUB), MOE_SUB)
            gu = jnp.dot(xbf[lo, :], wgu_bf[...], preferred_element_type=F32) + bgu
            nxt = pltpu.roll(gu, 2 * MOE_CHUNK - 1, 1)
            gate = jnp.minimum(gu, SWIGLU_LIMIT)
            up = jnp.clip(nxt, -SWIGLU_LIMIT, SWIGLU_LIMIT)
            hid = (gate * jax.nn.sigmoid(SWIGLU_ALPHA * gate) * (up + 1.0)).astype(BF16)
            hid = jnp.dot(hid, psel_ref[...], preferred_element_type=F32).astype(BF16)
            acc[lo, :] += jnp.dot(hid, wd_bf[...], preferred_element_type=F32)

        for_subs(compute)

        @pl.when(j == pl.num_programs(1) - 1)
        def _():
            for_subs(lambda i: sub_copy(acc, out_hbm, i, True).start())
            for_subs(lambda i: sub_copy(acc, out_hbm, i, True).wait())


def _moe(e_tbl, row_tbl, ns_tbl, tail_tbl, xs, w_gate_up, b_gate_up, w_down, b_down):
    nj = D_EXPERT // MOE_CHUNK
    last = nj - 1
    r = np.arange(2 * MOE_CHUNK)
    psel = jnp.asarray((r[:, None] == 2 * np.arange(MOE_CHUNK)[None, :]).astype(np.float32)).astype(BF16)

    def chunk(j, ns, sb):
        return jnp.where(ns[sb] > 0, j, last)

    return pl.pallas_call(
        _moe_kernel,
        grid_spec=pltpu.PrefetchScalarGridSpec(
            num_scalar_prefetch=4,
            grid=(MOE_NSB, nj),
            in_specs=[pl.BlockSpec(memory_space=pl.ANY),
                      pl.BlockSpec((1, D_MODEL, 2 * MOE_CHUNK), lambda sb, j, e, r, ns, tl: (e[sb], 0, chunk(j, ns, sb))),
                      pl.BlockSpec((1, 1, 2 * MOE_CHUNK), lambda sb, j, e, r, ns, tl: (e[sb], 0, chunk(j, ns, sb))),
                      pl.BlockSpec((1, MOE_CHUNK, D_MODEL), lambda sb, j, e, r, ns, tl: (e[sb], chunk(j, ns, sb), 0)),
                      pl.BlockSpec((1, 1, D_MODEL), lambda sb, j, e, r, ns, tl: (e[sb], 0, 0)),
                      pl.BlockSpec((2 * MOE_CHUNK, MOE_CHUNK), lambda sb, j, e, r, ns, tl: (0, 0))],
            out_specs=pl.BlockSpec(memory_space=pl.ANY),
            scratch_shapes=[pltpu.VMEM((MOE_SUPER, D_MODEL), F32),
                            pltpu.VMEM((MOE_SUPER, D_MODEL), BF16),
                            pltpu.VMEM((MOE_SUPER, D_MODEL), F32),
                            pltpu.VMEM((D_MODEL, 2 * MOE_CHUNK), BF16),
                            pltpu.VMEM((MOE_CHUNK, D_MODEL), BF16),
                            pltpu.SemaphoreType.DMA(())]),
        out_shape=jax.ShapeDtypeStruct((MOE_CAP, D_MODEL), F32),
        compiler_params=_params(("arbitrary", "arbitrary")),
        name="moe_experts",
    )(e_tbl, row_tbl, ns_tbl, tail_tbl, xs, w_gate_up, b_gate_up.reshape(N_EXPERTS, 1, -1), w_down,
      b_down.reshape(N_EXPERTS, 1, -1), psel)


def _combine_kernel(dest_ref, out_hbm, x1_ref, w_ref, g2_ref, y_ref, rows, sem, *, tt):
    i = pl.program_id(0)

    def copy(t, k):
        return pltpu.make_async_copy(out_hbm.at[pl.ds(dest_ref[k * T_ALL + i * tt + t], 1)],
                                     rows.at[k, pl.ds(t, 1)], sem)

    def issue(t, c):
        for k in range(TOP_K):
            copy(t, k).start()
        return c

    lax.fori_loop(0, tt, issue, 0)

    def drain(t, c):
        for k in range(TOP_K):
            copy(t, k).wait()
        return c

    lax.fori_loop(0, tt, drain, 0)
    moe = rows[0] * w_ref[:, 0:1]
    for k in range(1, TOP_K):
        moe = moe + rows[k] * w_ref[:, k:k + 1]
    y_ref[...] = x1_ref[...] + g2_ref[0] * moe


def _combine(dest_flat, out_rows, x1, w_tok, mod3):
    tt = 256
    per = GROUP_ROWS // tt
    return pl.pallas_call(
        functools.partial(_combine_kernel, tt=tt),
        grid_spec=pltpu.PrefetchScalarGridSpec(
            num_scalar_prefetch=1,
            grid=(T_ALL // tt,),
            in_specs=[pl.BlockSpec(memory_space=pl.ANY),
                      pl.BlockSpec((tt, D_MODEL), lambda i, d: (i, 0)),
                      pl.BlockSpec((tt, TOP_K), lambda i, d: (i, 0)),
                      pl.BlockSpec((1, 1, D_MODEL), lambda i, d: ((i // per) * N_MOD + 5, 0, 0))],
            out_specs=pl.BlockSpec((tt, D_MODEL), lambda i, d: (i, 0)),
            scratch_shapes=[pltpu.VMEM((TOP_K, tt, D_MODEL), F32), pltpu.SemaphoreType.DMA(())]),
        out_shape=jax.ShapeDtypeStruct((T_ALL, D_MODEL), F32),
        compiler_params=_params(("arbitrary",)),
        name="combine",
    )(dest_flat, out_rows, x1, w_tok, mod3)


def _superblock_tables(counts):
    nsub = (counts + MOE_SUB - 1) // MOE_SUB
    start = (jnp.cumsum(nsub) - nsub) * MOE_SUB
    nsb = (nsub + MOE_SUPER_SUBS - 1) // MOE_SUPER_SUBS
    sb_end = jnp.cumsum(nsb)
    sb_start = sb_end - nsb
    sb = jnp.arange(MOE_NSB, dtype=I32)
    e_of = jnp.minimum(jnp.searchsorted(sb_end, sb, side="right"), N_EXPERTS - 1).astype(I32)
    valid = sb < sb_end[-1]
    part = sb - sb_start[e_of]
    row = jnp.where(valid, start[e_of] + part * MOE_SUPER, 0).astype(I32)
    ns = jnp.where(valid, jnp.clip(nsub[e_of] - part * MOE_SUPER_SUBS, 0, MOE_SUPER_SUBS), 0).astype(I32)
    e_last = e_of[jnp.maximum(sb_end[-1] - 1, 0)]
    first_unused = jnp.sum(nsub).astype(I32).reshape(1)
    return jnp.where(valid, e_of, e_last).astype(I32), row, ns, first_unused


def _rope_tables():
    rows = DEC_SEQ // GRID_W
    row = jnp.repeat(jnp.arange(rows, dtype=F32), GRID_W)
    col = jnp.tile(jnp.arange(GRID_W, dtype=F32), rows)
    n_freq = QK_ROPE // 4
    inv_freq = jnp.power(ROPE_THETA, -jnp.arange(n_freq, dtype=F32) / n_freq)
    ang = jnp.concatenate([row[:, None] * inv_freq, col[:, None] * inv_freq], axis=-1)
    ang = jnp.concatenate([ang, ang], axis=-1)
    zeros = jnp.zeros((DEC_SEQ, LANE - QK_ROPE), F32)
    cos_l = jnp.concatenate([jnp.cos(ang), zeros], axis=-1)
    sin_l = jnp.concatenate([jnp.sin(ang), zeros], axis=-1)
    cos = jnp.concatenate([jnp.ones((T_CTX, LANE), F32)] + [cos_l] * DEC_BATCH, axis=0)
    sin = jnp.concatenate([jnp.zeros((T_CTX, LANE), F32)] + [sin_l] * DEC_BATCH, axis=0)
    return cos, sin


def _pad_lanes(w, width):
    return jnp.concatenate([w, jnp.zeros(w.shape[:-1] + (width - w.shape[-1],), w.dtype)], axis=-1)


def kernel(x_prompt, x_sample, cache_ckv, cache_kpe, c, c_ctx, w_mod, b_mod, norm1_w, norm2_w, w_in, hy_conv_w, hy_conv_b, filt_w1, filt_b1, filt_w2, filt_b2, filt_w3, filt_b3, filt_freq, hy_skip, q_a_norm_w, w_uq, kv_a_norm_w, w_ukv, qn_norm_w, kn_norm_w, qr_norm_w, kr_norm_w, w_hy_out, w_mla_out, w_o, w_router, b_router, w_gate_up, b_gate_up, w_down, b_down):
    l = 0
    x = jnp.concatenate([x_prompt.reshape(T_CTX, D_MODEL), x_sample.reshape(T_LAT, D_MODEL)], axis=0)

    cvec = jnp.concatenate([c_ctx[None, :], c, jnp.zeros((8 - 1 - DEC_BATCH, D_MODEL), F32)], axis=0)
    mod = _modulation(cvec, w_mod[l], b_mod[l])
    mod3 = mod[:N_GROUPS].reshape(N_GROUPS * N_MOD, 1, D_MODEL)

    h1 = _prenorm(x, norm1_w[l], mod3, 0, 1)

    w_in_l = w_in[l]
    c_u3 = 3 * D_HYENA
    c_lat = c_u3 + Q_LORA + KV_LORA + QK_ROPE
    w_u3 = w_in_l[:, :c_u3].astype(BF16)
    w_lat = _pad_lanes(w_in_l[:, c_u3:c_lat], LAT_COLS).astype(BF16)
    w_gates = w_in_l[:, c_lat:].astype(BF16)
    u3 = _matmul(h1, w_u3, tm=1024, tn=1024, out_dtype=F32, name="in_proj_hyena")
    lat = _matmul(h1, w_lat, tm=1024, tn=LAT_COLS, out_dtype=F32, name="in_proj_latent")
    gates = _matmul(h1, w_gates, tm=1024, tn=1024, out_dtype=F32, act="sigmoid", name="in_proj_gates")

    max_decay = math.log(DECAY_TARGET) / SHORT_DECAY_PCT
    min_decay = math.log(DECAY_TARGET) / LONG_DECAY_PCT
    lp = {
        "bands": jnp.linspace(1e-4, FILTER_BANDS - 1, FILTER_BANDS, dtype=F32)[None, :],
        "deltas": jnp.linspace(min_decay, max_decay, D_HYENA, dtype=F32)[None, :],
        "filt_w1": filt_w1[l], "filt_b1": filt_b1[l], "filt_w2": filt_w2[l], "filt_b2": filt_b2[l],
        "filt_w3": filt_w3[l], "filt_b3": filt_b3[l], "filt_freq": filt_freq[l],
        "hy_conv_w": hy_conv_w[l], "hy_conv_b": hy_conv_b[l].reshape(1, -1), "hy_skip": hy_skip[l].reshape(2, 1, D_HYENA),
    }
    yh_ctx = _hyena_group(u3[:T_CTX].reshape(BATCH, SEQ, c_u3), SEQ, LANE, lp)
    yh_lat = _hyena_group(u3[T_CTX:].reshape(DEC_BATCH, DEC_SEQ, c_u3), DEC_SEQ, LANE, lp)

    wq = w_uq[l].reshape(Q_LORA, N_HEADS, QK_HEAD)
    wuq_p = jnp.concatenate([wq[..., :QK_NOPE].reshape(Q_LORA, -1),
                             _pad_lanes(wq[..., QK_NOPE:], LANE).reshape(Q_LORA, -1)], axis=-1).astype(BF16)
    wkv = w_ukv[l].reshape(KV_LORA, N_HEADS, QK_NOPE + V_HEAD)
    wukv_p = jnp.concatenate([wkv[..., :QK_NOPE].reshape(KV_LORA, -1),
                              wkv[..., QK_NOPE:].reshape(KV_LORA, -1)], axis=-1).astype(BF16)
    cos_t, sin_t = _rope_tables()
    q_all, ckv, kpe = _qkv(
        lat, q_a_norm_w[l].reshape(1, -1), wuq_p, jnp.tile(qn_norm_w[l], N_HEADS)[None, :],
        _pad_lanes(qr_norm_w[l], LANE)[None, :], kv_a_norm_w[l].reshape(1, -1),
        _pad_lanes(kr_norm_w[l], LANE)[None, :], cos_t, sin_t)
    kn_w = jnp.tile(kn_norm_w[l], N_HEADS)[None, :]
    k_ctx, v_ctx = _kvup(ckv[:T_CTX], wukv_p, kn_w)
    ckv_lat = jnp.concatenate([ckv[T_CTX:].reshape(DEC_BATCH, DEC_SEQ, KV_LORA), cache_ckv[:, l]], axis=1)
    k_lat, v_lat = _kvup(ckv_lat.reshape(DEC_BATCH * LK_LAT, KV_LORA), wukv_p, kn_w)
    kpe_lat = jnp.concatenate([kpe[T_CTX:].reshape(DEC_BATCH, DEC_SEQ, LANE),
                               _pad_lanes(cache_kpe[:, l], LANE)], axis=1)
    kpe_lat = kpe_lat.reshape(DEC_BATCH * LK_LAT, LANE).astype(BF16)
    ym_ctx = _attn_ctx(q_all, k_ctx, v_ctx, kpe[:T_CTX].astype(BF16))
    ym_lat = _attn_lat(q_all, k_lat, v_lat, kpe_lat)

    merged = _merge(yh_ctx, yh_lat, ym_ctx, ym_lat, gates, w_hy_out[l].astype(BF16), w_mla_out[l].astype(BF16))
    wr_t = w_router[l].T
    wr_hi = wr_t.astype(BF16)
    wr_lo = (wr_t - wr_hi.astype(F32)).astype(BF16)
    br = jnp.broadcast_to(b_router[l][:, None], (N_EXPERTS, LANE))
    x1, h2, logits_t = _oproj(merged, x, w_o[l].astype(BF16), mod3, norm2_w[l], wr_hi, wr_lo, br)

    w_top, dest, counts, starts = _route(logits_t)
    counts = counts[:, 0]
    starts = starts[:, 0]
    dest_flat = dest.reshape(-1)
    xs = _dispatch(dest_flat, counts, starts, h2)
    e_tbl, row_tbl, ns_tbl, tail_tbl = _superblock_tables(counts)
    out_rows = _moe(e_tbl, row_tbl, ns_tbl, tail_tbl, xs, w_gate_up[l], b_gate_up[l], w_down[l], b_down[l])
    y = _combine(dest_flat, out_rows, x1, w_top.T, mod3)

    y_p = y[:T_CTX].reshape(BATCH, SEQ, D_MODEL)
    y_s = y[T_CTX:].reshape(DEC_BATCH, DEC_SEQ, D_MODEL)
    new_ckv = ckv[:T_CTX].reshape(BATCH, 1, SEQ, KV_LORA)
    new_kpe = kpe[:T_CTX, :QK_ROPE].reshape(BATCH, 1, SEQ, QK_ROPE)
    return (y_p, y_s, new_ckv, new_kpe)
```

```python
import functools
import math

import numpy as np
import jax
import jax.numpy as jnp
from jax import lax
from jax.experimental import pallas as pl
from jax.experimental.pallas import tpu as pltpu

F32 = jnp.float32
BF16 = jnp.bfloat16
I32 = jnp.int32

D_MODEL = 2048
BATCH = 16
SEQ = 256
DEC_BATCH = 2
DEC_SEQ = 4096
PAST_LEN = 256
GRID_W = 64
EPS = 1e-6
N_MOD = 6

D_HYENA = 1024
SHORT_CONV = 3
FILTER_BANDS = 16
FILTER_HIDDEN = 64
DECAY_TARGET = 1e-2
SHORT_DECAY_PCT = 0.3
LONG_DECAY_PCT = 1.5

N_HEADS = 16
QK_NOPE = 128
QK_ROPE = 64
QK_HEAD = QK_NOPE + QK_ROPE
V_HEAD = 128
Q_LORA = 512
KV_LORA = 256
ROPE_THETA = 10000.0

N_EXPERTS = 32
TOP_K = 4
D_EXPERT = 2048
SWIGLU_LIMIT = 7.0
SWIGLU_ALPHA = 1.702

T_CTX = BATCH * SEQ
T_LAT = DEC_BATCH * DEC_SEQ
T_ALL = T_CTX + T_LAT
GROUP_ROWS = 4096
N_GROUPS = T_ALL // GROUP_ROWS
LK_LAT = DEC_SEQ + PAST_LEN

LANE = 128
LAT_COLS = 896

DFT_N1 = 256
PLANE_ROWS = DFT_N1 // 2

MOE_SUB = 256
MOE_SUPER_SUBS = 4
MOE_SUPER = MOE_SUB * MOE_SUPER_SUBS
MOE_CHUNK = 256
N_ASSIGN = T_ALL * TOP_K
MOE_CAP = N_ASSIGN + N_EXPERTS * MOE_SUB
MOE_NSB = (N_ASSIGN // MOE_SUB + N_EXPERTS) // MOE_SUPER_SUBS + N_EXPERTS
ROUTE_TILE = 512

VMEM_LIMIT = 56 * 1024 * 1024


def _params(sem, vmem=None):
    return pltpu.CompilerParams(dimension_semantics=sem, vmem_limit_bytes=vmem or VMEM_LIMIT)


def _rms(x, n=None):
    ms = jnp.sum(x * x, axis=-1, keepdims=True) * (1.0 / (n or x.shape[-1]))
    return x * lax.rsqrt(ms + EPS)


def _dot_nt(a, b):
    return lax.dot_general(a, b, (((1,), (1,)), ((), ())), preferred_element_type=F32)


def _dot_hi(a, b):
    return jnp.dot(a, b, preferred_element_type=F32, precision=lax.Precision.HIGHEST)


def _mod_kernel(c_ref, w_ref, b_ref, o_ref):
    c = c_ref[...]
    a = (c * jax.nn.sigmoid(c)).astype(BF16)
    o_ref[...] = jnp.dot(a, w_ref[...].astype(BF16), preferred_element_type=F32) + b_ref[...]


def _modulation(cvec8, w_mod, b_mod):
    tn = 1024
    n = w_mod.shape[1]
    return pl.pallas_call(
        _mod_kernel,
        grid=(n // tn,),
        in_specs=[pl.BlockSpec((8, D_MODEL), lambda j: (0, 0)),
                  pl.BlockSpec((D_MODEL, tn), lambda j: (0, j)),
                  pl.BlockSpec((1, tn), lambda j: (0, j))],
        out_specs=pl.BlockSpec((8, tn), lambda j: (0, j)),
        out_shape=jax.ShapeDtypeStruct((8, n), F32),
        compiler_params=_params(("arbitrary",)),
        name="modulation",
    )(cvec8, w_mod, b_mod.reshape(1, n))


def _two_group_specs(tm):
    nc = T_CTX // tm
    return (pl.BlockSpec((tm, D_MODEL), lambda i, *_: (jnp.minimum(i, nc - 1), 0)),
            pl.BlockSpec((tm, D_MODEL), lambda i, *_: (jnp.maximum(i - nc, 0), 0)), nc)


def _for_group(ctx_tiles, run, ctx_ref, lat_ref):
    is_ctx = pl.program_id(0) < ctx_tiles
    pl.when(is_ctx)(lambda: run(ctx_ref))
    pl.when(jnp.logical_not(is_ctx))(lambda: run(lat_ref))


def _prenorm_kernel(xc_ref, xl_ref, nw_ref, sh_ref, sc_ref, o_ref, *, ctx_tiles):
    def run(x_ref):
        y = _rms(x_ref[...]) * nw_ref[...]
        o_ref[...] = (y * (1.0 + sc_ref[0]) + sh_ref[0]).astype(o_ref.dtype)

    _for_group(ctx_tiles, run, xc_ref, xl_ref)


def _prenorm(x_ctx, x_lat, norm_w, mod3, shift_idx, scale_idx):
    tm = 512
    per = GROUP_ROWS // tm
    ctx_spec, lat_spec, nc = _two_group_specs(tm)
    return pl.pallas_call(
        functools.partial(_prenorm_kernel, ctx_tiles=nc),
        grid=(T_ALL // tm,),
        in_specs=[ctx_spec, lat_spec,
                  pl.BlockSpec((1, D_MODEL), lambda i: (0, 0)),
                  pl.BlockSpec((1, 1, D_MODEL), lambda i: ((i // per) * N_MOD + shift_idx, 0, 0)),
                  pl.BlockSpec((1, 1, D_MODEL), lambda i: ((i // per) * N_MOD + scale_idx, 0, 0))],
        out_specs=pl.BlockSpec((tm, D_MODEL), lambda i: (i, 0)),
        out_shape=jax.ShapeDtypeStruct((T_ALL, D_MODEL), BF16),
        compiler_params=_params(("arbitrary",)),
        name="prenorm",
    )(x_ctx, x_lat, norm_w.reshape(1, D_MODEL), mod3, mod3)


def _mm_kernel(a_ref, b_ref, o_ref, *, act):
    r = jnp.dot(a_ref[...], b_ref[...], preferred_element_type=F32)
    if act == "sigmoid":
        r = jax.nn.sigmoid(r)
    o_ref[...] = r.astype(o_ref.dtype)


def _matmul(a, b, *, tm, tn, out_dtype, act=None, name="matmul"):
    m, k = a.shape
    n = b.shape[1]
    return pl.pallas_call(
        functools.partial(_mm_kernel, act=act),
        grid=(n // tn, m // tm),
        in_specs=[pl.BlockSpec((tm, k), lambda j, i: (i, 0)),
                  pl.BlockSpec((k, tn), lambda j, i: (0, j))],
        out_specs=pl.BlockSpec((tm, tn), lambda j, i: (i, j)),
        out_shape=jax.ShapeDtypeStruct((m, n), out_dtype),
        compiler_params=_params(("arbitrary", "arbitrary")),
        name=name,
    )(a, b)


def _rope128(y, cos, sin, lane):
    rot = jnp.where(lane < QK_ROPE // 2, -pltpu.roll(y, LANE - QK_ROPE // 2, 1), pltpu.roll(y, QK_ROPE // 2, 1))
    return y * cos + rot * sin


def _qkv_kernel(lat_ref, qa_ref, wuq_ref, qn_ref, qr_ref, kva_ref, kr_ref, cos_ref, sin_ref,
                q_ref, ckv_ref, kpe_ref):
    lat = lat_ref[...]
    cos = cos_ref[...]
    sin = sin_ref[...]
    lane = lax.broadcasted_iota(I32, cos.shape, 1)
    qc = (_rms(lat[:, :Q_LORA]) * qa_ref[...]).astype(BF16)
    q = jnp.dot(qc, wuq_ref[...], preferred_element_type=F32)
    scale = QK_HEAD ** -0.5 * math.log2(math.e)
    nope_cols = N_HEADS * QK_NOPE
    for h in range(N_HEADS):
        sl = slice(h * LANE, (h + 1) * LANE)
        q_ref[:, 2 * h * LANE:(2 * h + 1) * LANE] = (_rms(q[:, sl]) * qn_ref[:, sl] * scale).astype(BF16)
        sr = slice(nope_cols + h * LANE, nope_cols + (h + 1) * LANE)
        yr = _rms(q[:, sr], QK_ROPE) * qr_ref[...]
        q_ref[:, (2 * h + 1) * LANE:(2 * h + 2) * LANE] = (_rope128(yr, cos, sin, lane) * scale).astype(BF16)
    ckv_ref[...] = _rms(lat[:, Q_LORA:Q_LORA + KV_LORA]) * kva_ref[...]
    kp = _rms(lat[:, Q_LORA + KV_LORA:], QK_ROPE) * kr_ref[...]
    kpe_ref[...] = _rope128(kp, cos, sin, lane)


def _qkv(lat, qa_w, wuq_p, qn_w, qr_w, kva_w, kr_w, cos_t, sin_t):
    tm = 256
    qcols = 2 * N_HEADS * LANE
    row = lambda i: (i, 0)
    fix = lambda i: (0, 0)
    return pl.pallas_call(
        _qkv_kernel,
        grid=(T_ALL // tm,),
        in_specs=[pl.BlockSpec((tm, LAT_COLS), row),
                  pl.BlockSpec((1, Q_LORA), fix),
                  pl.BlockSpec((Q_LORA, qcols), fix),
                  pl.BlockSpec((1, N_HEADS * LANE), fix),
                  pl.BlockSpec((1, LANE), fix),
                  pl.BlockSpec((1, KV_LORA), fix),
                  pl.BlockSpec((1, LANE), fix),
                  pl.BlockSpec((tm, LANE), row),
                  pl.BlockSpec((tm, LANE), row)],
        out_specs=[pl.BlockSpec((tm, qcols), row),
                   pl.BlockSpec((tm, KV_LORA), row),
                   pl.BlockSpec((tm, LANE), row)],
        out_shape=[jax.ShapeDtypeStruct((T_ALL, qcols), BF16),
                   jax.ShapeDtypeStruct((T_ALL, KV_LORA), F32),
                   jax.ShapeDtypeStruct((T_ALL, LANE), F32)],
        compiler_params=_params(("arbitrary",)),
        name="mla_qkv",
    )(lat, qa_w, wuq_p, qn_w, qr_w, kva_w, kr_w, cos_t, sin_t)


def _kvup_kernel(c_ref, w_ref, kn_ref, k_ref, v_ref):
    kv = jnp.dot(c_ref[...].astype(BF16), w_ref[...], preferred_element_type=F32)
    half = N_HEADS * LANE
    for h in range(N_HEADS):
        sl = slice(h * LANE, (h + 1) * LANE)
        k_ref[:, sl] = (_rms(kv[:, sl]) * kn_ref[:, sl]).astype(BF16)
    v_ref[...] = kv[:, half:].astype(BF16)


def _kvup(ckv_rows, wukv_p, kn_w):
    tm = 256
    rows = ckv_rows.shape[0]
    half = N_HEADS * LANE
    return pl.pallas_call(
        _kvup_kernel,
        grid=(rows // tm,),
        in_specs=[pl.BlockSpec((tm, KV_LORA), lambda i: (i, 0)),
                  pl.BlockSpec((KV_LORA, 2 * half), lambda i: (0, 0)),
                  pl.BlockSpec((1, half), lambda i: (0, 0))],
        out_specs=[pl.BlockSpec((tm, half), lambda i: (i, 0)),
                   pl.BlockSpec((tm, half), lambda i: (i, 0))],
        out_shape=[jax.ShapeDtypeStruct((rows, half), BF16),
                   jax.ShapeDtypeStruct((rows, half), BF16)],
        compiler_params=_params(("arbitrary",)),
        name="mla_kv_up",
    )(ckv_rows, wukv_p, kn_w)


def _attn_ctx_kernel(q_ref, k_ref, v_ref, kp_ref, o_ref):
    kp = kp_ref[...]
    for h in range(N_HEADS):
        sl = slice(h * LANE, (h + 1) * LANE)
        s = (_dot_nt(q_ref[:, 2 * h * LANE:(2 * h + 1) * LANE], k_ref[:, sl])
             + _dot_nt(q_ref[:, (2 * h + 1) * LANE:(2 * h + 2) * LANE], kp))
        p = jnp.exp2(s - jnp.max(s, axis=-1, keepdims=True))
        l = jnp.sum(p, axis=-1, keepdims=True)
        o = jnp.dot(p.astype(BF16), v_ref[:, sl], preferred_element_type=F32)
        o_ref[:, sl] = (o / l).astype(BF16)


def _attn_ctx(q_all, k_ctx, v_ctx, kpe_ctx):
    half = N_HEADS * LANE
    row = lambda b: (b, 0)
    return pl.pallas_call(
        _attn_ctx_kernel,
        grid=(BATCH,),
        in_specs=[pl.BlockSpec((SEQ, 2 * half), row),
                  pl.BlockSpec((SEQ, half), row),
                  pl.BlockSpec((SEQ, half), row),
                  pl.BlockSpec((SEQ, LANE), row)],
        out_specs=pl.BlockSpec((SEQ, half), row),
        out_shape=jax.ShapeDtypeStruct((T_CTX, half), BF16),
        compiler_params=_params(("arbitrary",)),
        name="attn_ctx",
    )(q_all, k_ctx, v_ctx, kpe_ctx)


ATTN_KV_CHUNK = 256


def _attn_lat_kernel(q_ref, kn_ref, kp_ref, v_ref, o_ref, k_scr, s_scr):
    @pl.when(pl.program_id(2) == 0)
    def _():
        k_scr[:, :LANE] = kn_ref[...]
        k_scr[:, LANE:] = kp_ref[...]

    q = q_ref[...]
    chunks = [slice(c * ATTN_KV_CHUNK, (c + 1) * ATTN_KV_CHUNK) for c in range(LK_LAT // ATTN_KV_CHUNK)]
    def fold(x):
        return [x[:, t * LANE:(t + 1) * LANE] for t in range(ATTN_KV_CHUNK // LANE)]

    m_part = None
    for ck in chunks:
        s = _dot_nt(q, k_scr[ck, :])
        s_scr[:, ck] = s
        for part in fold(s):
            m_part = part if m_part is None else jnp.maximum(m_part, part)
    m = jnp.max(m_part, axis=-1, keepdims=True)
    l_part = None
    acc = None
    for ck in chunks:
        p = jnp.exp2(s_scr[:, ck] - m)
        for part in fold(p):
            l_part = part if l_part is None else l_part + part
        pv = jnp.dot(p.astype(BF16), v_ref[ck, :], preferred_element_type=F32)
        acc = pv if acc is None else acc + pv
    l = jnp.sum(l_part, axis=-1, keepdims=True)
    o_ref[...] = (acc / l).astype(BF16)


def _attn_lat(q_all, k_lat, v_lat, kpe_lat):
    tq = 512
    nq = DEC_SEQ // tq
    off = T_CTX // tq
    return pl.pallas_call(
        _attn_lat_kernel,
        grid=(DEC_BATCH, N_HEADS, nq),
        in_specs=[pl.BlockSpec((tq, 2 * LANE), lambda b, h, i: (off + b * nq + i, h)),
                  pl.BlockSpec((LK_LAT, LANE), lambda b, h, i: (b, h)),
                  pl.BlockSpec((LK_LAT, LANE), lambda b, h, i: (b, 0)),
                  pl.BlockSpec((LK_LAT, LANE), lambda b, h, i: (b, h))],
        out_specs=pl.BlockSpec((tq, LANE), lambda b, h, i: (b * nq + i, h)),
        out_shape=jax.ShapeDtypeStruct((T_LAT, N_HEADS * V_HEAD), BF16),
        scratch_shapes=[pltpu.VMEM((LK_LAT, 2 * LANE), BF16), pltpu.VMEM((tq, LK_LAT), F32)],
        compiler_params=_params(("arbitrary", "arbitrary", "arbitrary")),
        name="attn_lat",
    )(q_all, k_lat, kpe_lat, v_lat)


def _dft_constants(n2):
    n = n2 * DFT_N1
    a = np.arange(PLANE_ROWS, dtype=np.float64)[None, None, :]
    p = np.arange(n2, dtype=np.float64)[:, None, None]
    k1 = np.arange(DFT_N1, dtype=np.float64)[None, :, None]
    theta = 2.0 * np.pi * ((n2 * a + p) * k1 % n) / n
    fwd = np.concatenate([np.cos(theta), -np.sin(theta)], axis=1)
    return fwd, np.transpose(fwd, (0, 2, 1))


def _radix_forward(y_ref, n2):
    m = n2 // 2
    while m >= 1:
        for j in range(m):
            ang = -2.0 * math.pi * j / (2 * m)
            wr, wi = math.cos(ang), math.sin(ang)

            def body(g, c, j=j, m=m, wr=wr, wi=wi):
                p = g * (2 * m) + j
                q = p + m
                a = y_ref[p]
                b = y_ref[q]
                y_ref[p] = a + b
                d = a - b
                if j == 0:
                    y_ref[q] = d
                else:
                    dr, di = d[:DFT_N1], d[DFT_N1:]
                    y_ref[q, :DFT_N1] = dr * wr - di * wi
                    y_ref[q, DFT_N1:] = dr * wi + di * wr
                return c

            lax.fori_loop(0, n2 // (2 * m), body, 0)
        m //= 2


def _radix_inverse(y_ref, n2):
    m = 1
    while m <= n2 // 2:
        for j in range(m):
            ang = 2.0 * math.pi * j / (2 * m)
            wr, wi = math.cos(ang), math.sin(ang)

            def body(g, c, j=j, m=m, wr=wr, wi=wi):
                p = g * (2 * m) + j
                q = p + m
                a = y_ref[p]
                b = y_ref[q]
                if j != 0:
                    br, bi = b[:DFT_N1], b[DFT_N1:]
                    b = jnp.concatenate([br * wr - bi * wi, br * wi + bi * wr], axis=0)
                y_ref[p] = a + b
                y_ref[q] = a - b
                return c

            lax.fori_loop(0, n2 // (2 * m), body, 0)
        m *= 2


def _filt_hidden_kernel(bands_ref, w1t_ref, w1c_ref, w1s_ref, b1_ref, w2_ref, b2_ref, fr_ref, o_ref, *, n):
    rows = o_ref.shape[0]
    pos = (lax.broadcasted_iota(I32, (rows, 1), 0) - 8).astype(F32)
    t = pos / (n - 1.0)
    bw = bands_ref[...] * ((2.0 * math.pi / n) * pos)
    z = t * w1t_ref[...] + _dot_hi(jnp.cos(bw), w1c_ref[...]) + _dot_hi(-jnp.sin(bw), w1s_ref[...]) + b1_ref[...]
    h = jnp.sin(fr_ref[0:1, :FILTER_HIDDEN] * z)
    o_ref[...] = jnp.sin(fr_ref[1:2, :] * (_dot_hi(h, w2_ref[...]) + b2_ref[...]))


def _filt_hidden(n, bands, w1, b1, w2, b2, freq):
    nb = FILTER_BANDS
    return pl.pallas_call(
        functools.partial(_filt_hidden_kernel, n=n),
        out_shape=jax.ShapeDtypeStruct((n + 8, LANE), F32),
        compiler_params=_params(None),
        name=f"filt_hidden_{n}",
    )(bands, w1[0:1], w1[1:1 + nb], w1[1 + nb:], b1.reshape(1, -1), _pad_lanes(w2, LANE),
      _pad_lanes(b2.reshape(1, -1), LANE), _pad_lanes(freq, LANE))


def _filt_spec_kernel(hdn_ref, w3f_ref, b3f_ref, w3b_ref, b3b_ref, dl_ref, fh_ref, o_ref, yb_ref, *, n, n2):
    inv_n = 1.0 / (2 * n)
    absd = jnp.abs(dl_ref[...])

    def plane(p, c):
        a = lax.broadcasted_iota(I32, (PLANE_ROWS, 1), 0)
        fh = fh_ref[p]

        def spectrum(shift, w3_ref, b3_ref):
            pos = a * n2 + p - shift
            rows = hdn_ref[pl.ds(8 + p - shift, PLANE_ROWS, stride=n2), :]
            t = pos.astype(F32) / (n - 1.0)
            h = (_dot_hi(rows, w3_ref[...]) + b3_ref[...]) * jnp.exp(-t * absd)
            h = jnp.where(pos >= 0, h, 0.0)
            return jnp.dot(fh, h.astype(BF16), preferred_element_type=F32)

        o_ref[0, p] = spectrum(0, w3f_ref, b3f_ref)
        yb_ref[p] = spectrum(1, w3b_ref, b3b_ref)
        return c

    lax.fori_loop(0, n2, plane, 0, unroll=2)
    _radix_forward(o_ref.at[0], n2)
    _radix_forward(yb_ref, n2)

    def combine(p, c):
        f = o_ref[0, p]
        b = yb_ref[p]
        o_ref[0, p, :DFT_N1] = (f[:DFT_N1] + b[:DFT_N1]) * inv_n
        o_ref[0, p, DFT_N1:] = (f[DFT_N1:] - b[DFT_N1:]) * inv_n
        return c

    lax.fori_loop(0, n2, combine, 0)


def _filt_spectrum(n, n2, hdn, w3, b3, deltas, f_hi):
    tc = LANE
    nct = D_HYENA // tc
    w3p = jnp.concatenate([w3, jnp.zeros((LANE - FILTER_HIDDEN, w3.shape[1]), F32)], axis=0)
    return pl.pallas_call(
        functools.partial(_filt_spec_kernel, n=n, n2=n2),
        grid=(2, nct),
        in_specs=[pl.BlockSpec((n + 8, LANE), lambda o, c: (0, 0)),
                  pl.BlockSpec((LANE, tc), lambda o, c: (0, o * 2 * nct + c)),
                  pl.BlockSpec((1, tc), lambda o, c: (0, o * 2 * nct + c)),
                  pl.BlockSpec((LANE, tc), lambda o, c: (0, o * 2 * nct + nct + c)),
                  pl.BlockSpec((1, tc), lambda o, c: (0, o * 2 * nct + nct + c)),
                  pl.BlockSpec((1, tc), lambda o, c: (0, c)),
                  pl.BlockSpec((n2, 2 * DFT_N1, PLANE_ROWS), lambda o, c: (0, 0, 0))],
        out_specs=pl.BlockSpec((1, n2, 2 * DFT_N1, tc), lambda o, c: (o, 0, 0, c)),
        out_shape=jax.ShapeDtypeStruct((2, n2, 2 * DFT_N1, D_HYENA), F32),
        scratch_shapes=[pltpu.VMEM((n2, 2 * DFT_N1, tc), F32)],
        compiler_params=_params(("arbitrary", "arbitrary")),
        name=f"filt_spectrum_{n}",
    )(hdn, w3p, b3.reshape(1, -1), w3p, b3.reshape(1, -1), deltas, f_hi)


def _hyena_kernel(*refs, n, n2, a_planes, out_dtype):
    if a_planes:
        a_ref, g_ref, gw_ref, gb_ref, kf_ref, sk_ref, f_ref, fi_ref, o_ref, y_ref, gpad = refs
    else:
        (a_ref, aw_ref, ab_ref, g_ref, gw_ref, gb_ref, kf_ref, sk_ref, f_ref, fi_ref, o_ref, y_ref, gpad, apad,
         a_scr) = refs
    tc = o_ref.shape[-1]
    zeros8 = jnp.zeros((8, tc), F32)

    def stage(pad_ref, src_ref):
        pad_ref[0:8, :] = zeros8
        pad_ref[n + 8:n + 16, :] = zeros8
        pad_ref[8:n + 8, :] = src_ref[0]

    def conv_plane(pad_ref, w_ref, b_ref, p):
        out = b_ref[...]
        for j in range(SHORT_CONV):
            out = out + pad_ref[pl.ds(7 + j + p, PLANE_ROWS, stride=n2), :] * w_ref[j:j + 1, :]
        return out

    stage(gpad, g_ref)
    if not a_planes:
        stage(apad, a_ref)

    def a_plane(p):
        return a_ref[0, p] if a_planes else a_scr[p]

    def fwd(p, c):
        if a_planes:
            plane = a_ref[0, p]
        else:
            plane = conv_plane(apad, aw_ref, ab_ref, p)
            a_scr[p] = plane
        y_ref[p] = jnp.dot(f_ref[p], plane.astype(BF16), preferred_element_type=F32)
        return c

    lax.fori_loop(0, n2, fwd, 0, unroll=min(n2, 4))
    _radix_forward(y_ref, n2)

    def mul(p, c):
        y = y_ref[p]
        k = kf_ref[0, p]
        yr, yi = y[:DFT_N1], y[DFT_N1:]
        kr, ki = k[:DFT_N1], k[DFT_N1:]
        y_ref[p, :DFT_N1] = yr * kr - yi * ki
        y_ref[p, DFT_N1:] = yr * ki + yi * kr
        return c

    lax.fori_loop(0, n2, mul, 0)
    _radix_inverse(y_ref, n2)

    def inv(p, c):
        conv = jnp.dot(fi_ref[p], y_ref[p].astype(BF16), preferred_element_type=F32)
        gate = conv_plane(gpad, gw_ref, gb_ref, p)
        o_ref[0, p] = (gate * (conv + sk_ref[0] * a_plane(p))).astype(out_dtype)
        return c

    lax.fori_loop(0, n2, inv, 0, unroll=min(n2, 4))


def _hyena_order(u3, a_src, a_col, g_col, conv_w, conv_b, kf, order, skip, f_fwd, f_inv, *,
                 n, n2, nb, b_off, tc, out_dtype):
    nct = D_HYENA // tc
    a_planes = a_src is not None
    tok = lambda col: pl.BlockSpec((1, n, tc), lambda c, b: (b + b_off, 0, col * nct + c))
    cw = lambda col: pl.BlockSpec((SHORT_CONV, tc), lambda c, b: (0, col * nct + c))
    cb = lambda col: pl.BlockSpec((1, tc), lambda c, b: (0, col * nct + c))
    planes = pl.BlockSpec((1, n2, PLANE_ROWS, tc), lambda c, b: (b, 0, 0, c))
    if a_planes:
        in_specs = [planes]
        args = [a_src]
    else:
        in_specs = [tok(a_col), cw(a_col), cb(a_col)]
        args = [u3, conv_w, conv_b]
    in_specs += [tok(g_col), cw(g_col), cb(g_col),
                 pl.BlockSpec((1, n2, 2 * DFT_N1, tc), lambda c, b: (order, 0, 0, c), pipeline_mode=pl.Buffered(1)),
                 pl.BlockSpec((1, 1, tc), lambda c, b: (order, 0, c)),
                 pl.BlockSpec((n2, 2 * DFT_N1, PLANE_ROWS), lambda c, b: (0, 0, 0), pipeline_mode=pl.Buffered(1)),
                 pl.BlockSpec((n2, PLANE_ROWS, 2 * DFT_N1), lambda c, b: (0, 0, 0), pipeline_mode=pl.Buffered(1))]
    args += [u3, conv_w, conv_b, kf, skip, f_fwd, f_inv]
    scratch = [pltpu.VMEM((n2, 2 * DFT_N1, tc), F32), pltpu.VMEM((n + 16, tc), F32)]
    if not a_planes:
        scratch += [pltpu.VMEM((n + 16, tc), F32), pltpu.VMEM((n2, PLANE_ROWS, tc), F32)]
    return pl.pallas_call(
        functools.partial(_hyena_kernel, n=n, n2=n2, a_planes=a_planes, out_dtype=out_dtype),
        grid=(nct, nb),
        in_specs=in_specs,
        out_specs=planes,
        out_shape=jax.ShapeDtypeStruct((nb, n2, PLANE_ROWS, D_HYENA), out_dtype),
        scratch_shapes=scratch,
        compiler_params=_params(("arbitrary", "arbitrary")),
        name=f"hyena_{n}_{'b' if a_planes else 'a'}",
    )(*args)


def _hyena_group(u3, n, nb, b_off, tc, lp):
    n2 = 2 * n // DFT_N1
    fwd, inv = _dft_constants(n2)
    f_fwd = jnp.asarray(fwd, F32).astype(BF16)
    f_inv = jnp.asarray(inv, F32).astype(BF16)
    hdn = _filt_hidden(n, lp["bands"], lp["filt_w1"], lp["filt_b1"], lp["filt_w2"], lp["filt_b2"], lp["filt_freq"])
    kf = _filt_spectrum(n, n2, hdn, lp["filt_w3"], lp["filt_b3"], lp["deltas"], f_fwd)
    common = dict(n=n, n2=n2, nb=nb, b_off=b_off, tc=tc)
    z = _hyena_order(u3, None, 0, 1, lp["hy_conv_w"], lp["hy_conv_b"], kf, 0, lp["hy_skip"], f_fwd, f_inv,
                     out_dtype=F32, **common)
    y = _hyena_order(u3, z, None, 2, lp["hy_conv_w"], lp["hy_conv_b"], kf, 1, lp["hy_skip"], f_fwd, f_inv,
                     out_dtype=BF16, **common)
    return jnp.transpose(y, (0, 2, 1, 3)).reshape(nb * n, D_HYENA)


def _merge_kernel(yhc_ref, yhl_ref, ymc_ref, yml_ref, g_ref, wh_ref, wm_ref, o_ref, *, ctx_tiles):
    def run(yh_ref, ym_ref):
        a = jnp.dot(yh_ref[...], wh_ref[...], preferred_element_type=F32)
        b = jnp.dot(ym_ref[...], wm_ref[...], preferred_element_type=F32)
        o_ref[...] = (g_ref[:, :D_MODEL] * a + g_ref[:, D_MODEL:] * b).astype(BF16)

    is_ctx = pl.program_id(0) < ctx_tiles
    pl.when(is_ctx)(lambda: run(yhc_ref, ymc_ref))
    pl.when(jnp.logical_not(is_ctx))(lambda: run(yhl_ref, yml_ref))


def _merge(yh_ctx, yh_lat, ym_ctx, ym_lat, gates, w_hy, w_mla):
    tm = 256
    nc = T_CTX // tm
    row = lambda i: (i, 0)
    fix = lambda i: (0, 0)
    ctx = lambda i: (jnp.minimum(i, nc - 1), 0)
    lat = lambda i: (jnp.maximum(i - nc, 0), 0)
    return pl.pallas_call(
        functools.partial(_merge_kernel, ctx_tiles=nc),
        grid=(T_ALL // tm,),
        in_specs=[pl.BlockSpec((tm, D_HYENA), ctx),
                  pl.BlockSpec((tm, D_HYENA), lat),
                  pl.BlockSpec((tm, N_HEADS * V_HEAD), ctx),
                  pl.BlockSpec((tm, N_HEADS * V_HEAD), lat),
                  pl.BlockSpec((tm, 2 * D_MODEL), row),
                  pl.BlockSpec((D_HYENA, D_MODEL), fix, pipeline_mode=pl.Buffered(1)),
                  pl.BlockSpec((N_HEADS * V_HEAD, D_MODEL), fix, pipeline_mode=pl.Buffered(1))],
        out_specs=pl.BlockSpec((tm, D_MODEL), row),
        out_shape=jax.ShapeDtypeStruct((T_ALL, D_MODEL), BF16),
        compiler_params=_params(("arbitrary",)),
        name="merge",
    )(yh_ctx, yh_lat, ym_ctx, ym_lat, gates, w_hy, w_mla)


def _oproj_kernel(m_ref, xc_ref, xl_ref, wo_ref, g1_ref, nw_ref, sh_ref, sc_ref, wrh_ref, wrl_ref, br_ref,
                  x1_ref, h2_ref, lg_ref, *, ctx_tiles):
    def run(x_ref):
        mix = jnp.dot(m_ref[...], wo_ref[...], preferred_element_type=F32)
        x1 = x_ref[...] + g1_ref[0] * mix
        x1_ref[...] = x1
        h2 = _rms(x1) * nw_ref[...] * (1.0 + sc_ref[0]) + sh_ref[0]
        h2_ref[...] = h2
        hh = h2.astype(BF16)
        hl = (h2 - hh.astype(F32)).astype(BF16)
        lg_ref[...] = (_dot_nt(wrh_ref[...], hh) + _dot_nt(wrh_ref[...], hl) + _dot_nt(wrl_ref[...], hh)
                       + br_ref[:, 0:1])

    _for_group(ctx_tiles, run, xc_ref, xl_ref)


def _oproj(merged, x_ctx, x_lat, w_o, mod3, norm2_w, wr_hi, wr_lo, b_router):
    tm = 256
    per = GROUP_ROWS // tm
    row = lambda i: (i, 0)
    fix = lambda i: (0, 0)
    modv = lambda k: pl.BlockSpec((1, 1, D_MODEL), lambda i: ((i // per) * N_MOD + k, 0, 0))
    ctx_spec, lat_spec, nc = _two_group_specs(tm)
    return pl.pallas_call(
        functools.partial(_oproj_kernel, ctx_tiles=nc),
        grid=(T_ALL // tm,),
        in_specs=[pl.BlockSpec((tm, D_MODEL), row),
                  ctx_spec, lat_spec,
                  pl.BlockSpec((D_MODEL, D_MODEL), fix, pipeline_mode=pl.Buffered(1)),
                  modv(2),
                  pl.BlockSpec((1, D_MODEL), fix),
                  modv(3), modv(4),
                  pl.BlockSpec((N_EXPERTS, D_MODEL), fix),
                  pl.BlockSpec((N_EXPERTS, D_MODEL), fix),
                  pl.BlockSpec((N_EXPERTS, LANE), fix)],
        out_specs=[pl.BlockSpec((tm, D_MODEL), row),
                   pl.BlockSpec((tm, D_MODEL), row),
                   pl.BlockSpec((N_EXPERTS, tm), lambda i: (0, i))],
        out_shape=[jax.ShapeDtypeStruct((T_ALL, D_MODEL), F32),
                   jax.ShapeDtypeStruct((T_ALL, D_MODEL), F32),
                   jax.ShapeDtypeStruct((N_EXPERTS, T_ALL), F32)],
        compiler_params=_params(("arbitrary",)),
        name="oproj_norm_router",
    )(merged, x_ctx, x_lat, w_o, mod3, norm2_w.reshape(1, D_MODEL), mod3, mod3, wr_hi, wr_lo, b_router)


def _route_kernel(lg_ref, tri_ref, low_ref, w_ref, dest_ref, cnt_ref, start_ref, idx_scr):
    nt = T_ALL // ROUTE_TILE
    eid = lax.broadcasted_iota(I32, (N_EXPERTS, ROUTE_TILE), 0).astype(F32)

    def tile(i, carry):
        sl = pl.ds(pl.multiple_of(i * ROUTE_TILE, ROUTE_TILE), ROUTE_TILE)
        vals = lg_ref[:, sl]
        mask = jnp.zeros((N_EXPERTS, ROUTE_TILE), F32)
        tops, hots = [], []
        for k in range(TOP_K):
            m = jnp.max(vals, axis=0, keepdims=True)
            sel = jnp.min(jnp.where(vals == m, eid, float(N_EXPERTS)), axis=0, keepdims=True)
            hot = eid == sel
            vals = jnp.where(hot, -jnp.inf, vals)
            mask = mask + hot.astype(F32)
            tops.append(m)
            hots.append(hot)
            idx_scr[k:k + 1, sl] = sel
        ex = [jnp.exp(t - tops[0]) for t in tops]
        den = ex[0] + ex[1] + ex[2] + ex[3]
        before = jnp.dot(mask.astype(BF16), tri_ref[...], preferred_element_type=F32) + carry
        for k in range(TOP_K):
            w_ref[k:k + 1, sl] = ex[k] / den
            rank = jnp.sum(jnp.where(hots[k], before, 0.0), axis=0, keepdims=True)
            dest_ref[k:k + 1, sl] = rank.astype(I32)
        return carry + jnp.sum(mask, axis=1, keepdims=True)

    counts = lax.fori_loop(0, nt, tile, jnp.zeros((N_EXPERTS, 1), F32))
    nsub = jnp.floor((counts + (MOE_SUB - 1.0)) * (1.0 / MOE_SUB))
    nsub_b = jnp.broadcast_to(nsub, (N_EXPERTS, LANE)).astype(BF16)
    start = jnp.dot(low_ref[...], nsub_b, preferred_element_type=F32) * float(MOE_SUB)
    cnt_ref[...] = jnp.broadcast_to(counts, (N_EXPERTS, LANE)).astype(I32)
    start_ref[...] = start.astype(I32)
    start_col = start[:, 0:1]

    def place(i, c):
        sl = pl.ds(pl.multiple_of(i * ROUTE_TILE, ROUTE_TILE), ROUTE_TILE)
        for k in range(TOP_K):
            hot = eid == idx_scr[k:k + 1, sl]
            base = jnp.sum(jnp.where(hot, start_col, 0.0), axis=0, keepdims=True)
            dest_ref[k:k + 1, sl] = dest_ref[k:k + 1, sl] + base.astype(I32)
        return c

    lax.fori_loop(0, nt, place, 0)


def _route(logits_t):
    r = np.arange(ROUTE_TILE)
    tri = jnp.asarray((r[:, None] < r[None, :]).astype(np.float32)).astype(BF16)
    e = np.arange(N_EXPERTS)
    low = jnp.asarray((e[None, :] < e[:, None]).astype(np.float32)).astype(BF16)
    return pl.pallas_call(
        _route_kernel,
        out_shape=[jax.ShapeDtypeStruct((TOP_K, T_ALL), F32),
                   jax.ShapeDtypeStruct((TOP_K, T_ALL), I32),
                   jax.ShapeDtypeStruct((N_EXPERTS, LANE), I32),
                   jax.ShapeDtypeStruct((N_EXPERTS, LANE), I32)],
        scratch_shapes=[pltpu.VMEM((TOP_K, T_ALL), F32)],
        compiler_params=_params(None),
        name="route",
    )(logits_t, tri, low)


def _row_copy(src, src_row, dst, dst_row, sem):
    return pltpu.make_async_copy(src.at[pl.ds(src_row, 1)], dst.at[pl.ds(dst_row, 1)], sem)


def _dispatch_kernel(dest_ref, cnt_ref, start_ref, h_ref, xs_hbm, zero_ref, sem, zsem, *, tt):
    i = pl.program_id(0)

    @pl.when(i == 0)
    def _():
        zero_ref[...] = jnp.zeros(zero_ref.shape, F32)

        def per_expert(e, c):
            cnt = cnt_ref[e]
            pad = (MOE_SUB - cnt % MOE_SUB) % MOE_SUB
            base = start_ref[e] + cnt

            def issue(r, cc):
                _row_copy(zero_ref, 0, xs_hbm, base + r, zsem).start()
                return cc

            lax.fori_loop(0, pad, issue, 0)

            def drain(r, cc):
                _row_copy(zero_ref, 0, xs_hbm, base + r, zsem).wait()
                return cc

            lax.fori_loop(0, pad, drain, 0)
            return c

        lax.fori_loop(0, N_EXPERTS, per_expert, 0)

        last = N_EXPERTS - 1
        first_unused = (start_ref[last] + cnt_ref[last] + MOE_SUB - 1) // MOE_SUB

        def tail(b):
            return pltpu.make_async_copy(zero_ref, xs_hbm.at[pl.ds(pl.multiple_of(b * MOE_SUB, MOE_SUB), MOE_SUB)], zsem)

        def tail_issue(b, c):
            tail(b).start()
            return c

        def tail_drain(b, c):
            tail(b).wait()
            return c

        lax.fori_loop(first_unused, MOE_CAP // MOE_SUB, tail_issue, 0)
        lax.fori_loop(first_unused, MOE_CAP // MOE_SUB, tail_drain, 0)

    def issue(t, c):
        for k in range(TOP_K):
            _row_copy(h_ref, t, xs_hbm, dest_ref[k * T_ALL + i * tt + t], sem).start()
        return c

    lax.fori_loop(0, tt, issue, 0, unroll=2)

    def drain(t, c):
        for k in range(TOP_K):
            _row_copy(h_ref, t, xs_hbm, dest_ref[k * T_ALL + i * tt + t], sem).wait()
        return c

    lax.fori_loop(0, tt, drain, 0, unroll=2)


def _dispatch(dest_flat, counts, starts, h2):
    tt = 512
    return pl.pallas_call(
        functools.partial(_dispatch_kernel, tt=tt),
        grid_spec=pltpu.PrefetchScalarGridSpec(
            num_scalar_prefetch=3,
            grid=(T_ALL // tt,),
            in_specs=[pl.BlockSpec((tt, D_MODEL), lambda i, d, c, s: (i, 0))],
            out_specs=pl.BlockSpec(memory_space=pl.ANY),
            scratch_shapes=[pltpu.VMEM((MOE_SUB, D_MODEL), F32), pltpu.SemaphoreType.DMA(()),
                            pltpu.SemaphoreType.DMA(())]),
        out_shape=jax.ShapeDtypeStruct((MOE_CAP, D_MODEL), F32),
        compiler_params=_params(("arbitrary",)),
        name="dispatch",
    )(dest_flat, counts, starts, h2)


def _moe_kernel(e_tbl, row_tbl, ns_tbl, tail_tbl, xs_hbm, wgu_ref, bgu_ref, wd_ref, bd_ref, psel_ref, out_hbm,
                x32, xbf, acc, wgu_bf, wd_bf, sem):
    del e_tbl
    sb = pl.program_id(0)
    j = pl.program_id(1)
    nsub = ns_tbl[sb]
    row0 = row_tbl[sb]

    @pl.when(jnp.logical_and(sb == 0, j == 0))
    def _():
        acc[0:MOE_SUB, :] = jnp.zeros((MOE_SUB, D_MODEL), F32)

        def tail(b):
            dst = out_hbm.at[pl.ds(pl.multiple_of(b * MOE_SUB, MOE_SUB), MOE_SUB)]
            return pltpu.make_async_copy(acc.at[pl.ds(0, MOE_SUB)], dst, sem)

        def tail_issue(b, c):
            tail(b).start()
            return c

        def tail_drain(b, c):
            tail(b).wait()
            return c

        lax.fori_loop(tail_tbl[0], MOE_CAP // MOE_SUB, tail_issue, 0)
        lax.fori_loop(tail_tbl[0], MOE_CAP // MOE_SUB, tail_drain, 0)

    def sub_copy(src, dst, i, to_hbm):
        lo = pl.ds(pl.multiple_of(i * MOE_SUB, MOE_SUB), MOE_SUB)
        hi = pl.ds(pl.multiple_of(row0 + i * MOE_SUB, MOE_SUB), MOE_SUB)
        if to_hbm:
            return pltpu.make_async_copy(src.at[lo], dst.at[hi], sem)
        return pltpu.make_async_copy(src.at[hi], dst.at[lo], sem)

    def for_subs(fn):
        def body(i, c):
            fn(i)
            return c
        lax.fori_loop(0, nsub, body, 0)

    @pl.when(nsub > 0)
    def _():
        @pl.when(j == 0)
        def _():
            for_subs(lambda i: sub_copy(xs_hbm, x32, i, False).start())
            for_subs(lambda i: sub_copy(xs_hbm, x32, i, False).wait())

            def init(i):
                lo = pl.ds(pl.multiple_of(i * MOE_SUB, MOE_SUB), MOE_SUB)
                xbf[lo, :] = x32[lo, :].astype(BF16)
                acc[lo, :] = jnp.broadcast_to(bd_ref[0], (MOE_SUB, D_MODEL))

            for_subs(init)

        wgu_bf[...] = wgu_ref[0].astype(BF16)
        wd_bf[...] = wd_ref[0].astype(BF16)
        bgu = bgu_ref[0]

        def compute(rows):
            lo = pl.ds(0, rows)
            gu = jnp.dot(xbf[lo, :], wgu_bf[...], preferred_element_type=F32) + bgu
            nxt = pltpu.roll(gu, 2 * MOE_CHUNK - 1, 1)
            gate = jnp.minimum(gu, SWIGLU_LIMIT)
            up = jnp.clip(nxt, -SWIGLU_LIMIT, SWIGLU_LIMIT)
            hid = (gate * jax.nn.sigmoid(SWIGLU_ALPHA * gate) * (up + 1.0)).astype(BF16)
            hid = jnp.dot(hid, psel_ref[...], preferred_element_type=F32).astype(BF16)
            acc[lo, :] += jnp.dot(hid, wd_bf[...], preferred_element_type=F32)

        for ns in range(1, MOE_SUPER_SUBS + 1):
            pl.when(nsub == ns)(functools.partial(compute, ns * MOE_SUB))

        @pl.when(j == pl.num_programs(1) - 1)
        def _():
            for_subs(lambda i: sub_copy(acc, out_hbm, i, True).start())
            for_subs(lambda i: sub_copy(acc, out_hbm, i, True).wait())


def _moe(e_tbl, row_tbl, ns_tbl, tail_tbl, xs, w_gate_up, b_gate_up, w_down, b_down):
    nj = D_EXPERT // MOE_CHUNK
    last = nj - 1
    r = np.arange(2 * MOE_CHUNK)
    psel = jnp.asarray((r[:, None] == 2 * np.arange(MOE_CHUNK)[None, :]).astype(np.float32)).astype(BF16)

    def chunk(j, ns, sb):
        return jnp.where(ns[sb] > 0, j, last)

    return pl.pallas_call(
        _moe_kernel,
        grid_spec=pltpu.PrefetchScalarGridSpec(
            num_scalar_prefetch=4,
            grid=(MOE_NSB, nj),
            in_specs=[pl.BlockSpec(memory_space=pl.ANY),
                      pl.BlockSpec((1, D_MODEL, 2 * MOE_CHUNK), lambda sb, j, e, r, ns, tl: (e[sb], 0, chunk(j, ns, sb))),
                      pl.BlockSpec((1, 1, 2 * MOE_CHUNK), lambda sb, j, e, r, ns, tl: (e[sb], 0, chunk(j, ns, sb))),
                      pl.BlockSpec((1, MOE_CHUNK, D_MODEL), lambda sb, j, e, r, ns, tl: (e[sb], chunk(j, ns, sb), 0)),
                      pl.BlockSpec((1, 1, D_MODEL), lambda sb, j, e, r, ns, tl: (e[sb], 0, 0)),
                      pl.BlockSpec((2 * MOE_CHUNK, MOE_CHUNK), lambda sb, j, e, r, ns, tl: (0, 0))],
            out_specs=pl.BlockSpec(memory_space=pl.ANY),
            scratch_shapes=[pltpu.VMEM((MOE_SUPER, D_MODEL), F32),
                            pltpu.VMEM((MOE_SUPER, D_MODEL), BF16),
                            pltpu.VMEM((MOE_SUPER, D_MODEL), F32),
                            pltpu.VMEM((D_MODEL, 2 * MOE_CHUNK), BF16),
                            pltpu.VMEM((MOE_CHUNK, D_MODEL), BF16),
                            pltpu.SemaphoreType.DMA(())]),
        out_shape=jax.ShapeDtypeStruct((MOE_CAP, D_MODEL), F32),
        compiler_params=_params(("arbitrary", "arbitrary")),
        name="moe_experts",
    )(e_tbl, row_tbl, ns_tbl, tail_tbl, xs, w_gate_up, b_gate_up.reshape(N_EXPERTS, 1, -1), w_down,
      b_down.reshape(N_EXPERTS, 1, -1), psel)


def _combine_kernel(dest_ref, out_hbm, x1_ref, w_ref, g2_ref, y_ref, rows, sem, *, tt, row0):
    i = pl.program_id(0)

    def copy(t, k):
        return pltpu.make_async_copy(out_hbm.at[pl.ds(dest_ref[k * T_ALL + row0 + i * tt + t], 1)],
                                     rows.at[k, pl.ds(t, 1)], sem)

    def issue(t, c):
        for k in range(TOP_K):
            copy(t, k).start()
        return c

    lax.fori_loop(0, tt, issue, 0, unroll=2)

    def drain(t, c):
        for k in range(TOP_K):
            copy(t, k).wait()
        return c

    lax.fori_loop(0, tt, drain, 0, unroll=2)
    moe = rows[0] * w_ref[:, 0:1]
    for k in range(1, TOP_K):
        moe = moe + rows[k] * w_ref[:, k:k + 1]
    y_ref[...] = x1_ref[...] + g2_ref[0] * moe


def _combine(dest_flat, out_rows, x1, w_tok, mod3, row0, nrows):
    tt = 256
    per = GROUP_ROWS // tt
    off = row0 // tt
    return pl.pallas_call(
        functools.partial(_combine_kernel, tt=tt, row0=row0),
        grid_spec=pltpu.PrefetchScalarGridSpec(
            num_scalar_prefetch=1,
            grid=(nrows // tt,),
            in_specs=[pl.BlockSpec(memory_space=pl.ANY),
                      pl.BlockSpec((tt, D_MODEL), lambda i, d: (off + i, 0)),
                      pl.BlockSpec((tt, TOP_K), lambda i, d: (off + i, 0)),
                      pl.BlockSpec((1, 1, D_MODEL), lambda i, d: (((off + i) // per) * N_MOD + 5, 0, 0))],
            out_specs=pl.BlockSpec((tt, D_MODEL), lambda i, d: (i, 0)),
            scratch_shapes=[pltpu.VMEM((TOP_K, tt, D_MODEL), F32), pltpu.SemaphoreType.DMA(())]),
        out_shape=jax.ShapeDtypeStruct((nrows, D_MODEL), F32),
        compiler_params=_params(("arbitrary",)),
        name="combine",
    )(dest_flat, out_rows, x1, w_tok, mod3)


def _superblock_tables(counts):
    nsub = (counts + MOE_SUB - 1) // MOE_SUB
    start = (jnp.cumsum(nsub) - nsub) * MOE_SUB
    nsb = (nsub + MOE_SUPER_SUBS - 1) // MOE_SUPER_SUBS
    sb_end = jnp.cumsum(nsb)
    sb_start = sb_end - nsb
    sb = jnp.arange(MOE_NSB, dtype=I32)
    e_of = jnp.minimum(jnp.sum((sb[:, None] >= sb_end[None, :]).astype(I32), axis=1), N_EXPERTS - 1)
    valid = sb < sb_end[-1]
    part = sb - sb_start[e_of]
    row = jnp.where(valid, start[e_of] + part * MOE_SUPER, 0).astype(I32)
    ns = jnp.where(valid, jnp.clip(nsub[e_of] - part * MOE_SUPER_SUBS, 0, MOE_SUPER_SUBS), 0).astype(I32)
    e_last = e_of[jnp.maximum(sb_end[-1] - 1, 0)]
    first_unused = jnp.sum(nsub).astype(I32).reshape(1)
    return jnp.where(valid, e_of, e_last).astype(I32), row, ns, first_unused


def _rope_tables():
    rows = DEC_SEQ // GRID_W
    row = jnp.repeat(jnp.arange(rows, dtype=F32), GRID_W)
    col = jnp.tile(jnp.arange(GRID_W, dtype=F32), rows)
    n_freq = QK_ROPE // 4
    inv_freq = jnp.power(ROPE_THETA, -jnp.arange(n_freq, dtype=F32) / n_freq)
    ang = jnp.concatenate([row[:, None] * inv_freq, col[:, None] * inv_freq], axis=-1)
    ang = jnp.concatenate([ang, ang], axis=-1)
    zeros = jnp.zeros((DEC_SEQ, LANE - QK_ROPE), F32)
    cos_l = jnp.concatenate([jnp.cos(ang), zeros], axis=-1)
    sin_l = jnp.concatenate([jnp.sin(ang), zeros], axis=-1)
    cos = jnp.concatenate([jnp.ones((T_CTX, LANE), F32)] + [cos_l] * DEC_BATCH, axis=0)
    sin = jnp.concatenate([jnp.zeros((T_CTX, LANE), F32)] + [sin_l] * DEC_BATCH, axis=0)
    return cos, sin


def _pad_lanes(w, width):
    return jnp.concatenate([w, jnp.zeros(w.shape[:-1] + (width - w.shape[-1],), w.dtype)], axis=-1)


def kernel(x_prompt, x_sample, cache_ckv, cache_kpe, c, c_ctx, w_mod, b_mod, norm1_w, norm2_w, w_in, hy_conv_w, hy_conv_b, filt_w1, filt_b1, filt_w2, filt_b2, filt_w3, filt_b3, filt_freq, hy_skip, q_a_norm_w, w_uq, kv_a_norm_w, w_ukv, qn_norm_w, kn_norm_w, qr_norm_w, kr_norm_w, w_hy_out, w_mla_out, w_o, w_router, b_router, w_gate_up, b_gate_up, w_down, b_down):
    l = 0
    x_ctx = x_prompt.reshape(T_CTX, D_MODEL)
    x_lat = x_sample.reshape(T_LAT, D_MODEL)

    cvec = jnp.concatenate([c_ctx[None, :], c, jnp.zeros((8 - 1 - DEC_BATCH, D_MODEL), F32)], axis=0)
    mod = _modulation(cvec, w_mod[l], b_mod[l])
    mod3 = mod[:N_GROUPS].reshape(N_GROUPS * N_MOD, 1, D_MODEL)

    h1 = _prenorm(x_ctx, x_lat, norm1_w[l], mod3, 0, 1)

    w_in_l = w_in[l]
    c_u3 = 3 * D_HYENA
    c_lat = c_u3 + Q_LORA + KV_LORA + QK_ROPE
    w_u3 = w_in_l[:, :c_u3].astype(BF16)
    w_lat = _pad_lanes(w_in_l[:, c_u3:c_lat], LAT_COLS).astype(BF16)
    w_gates = w_in_l[:, c_lat:].astype(BF16)
    u3 = _matmul(h1, w_u3, tm=1024, tn=1024, out_dtype=F32, name="in_proj_hyena")
    lat = _matmul(h1, w_lat, tm=1024, tn=LAT_COLS, out_dtype=F32, name="in_proj_latent")
    gates = _matmul(h1, w_gates, tm=1024, tn=1024, out_dtype=F32, act="sigmoid", name="in_proj_gates")

    max_decay = math.log(DECAY_TARGET) / SHORT_DECAY_PCT
    min_decay = math.log(DECAY_TARGET) / LONG_DECAY_PCT
    lp = {
        "bands": jnp.linspace(1e-4, FILTER_BANDS - 1, FILTER_BANDS, dtype=F32)[None, :],
        "deltas": jnp.linspace(min_decay, max_decay, D_HYENA, dtype=F32)[None, :],
        "filt_w1": filt_w1[l], "filt_b1": filt_b1[l], "filt_w2": filt_w2[l], "filt_b2": filt_b2[l],
        "filt_w3": filt_w3[l], "filt_b3": filt_b3[l], "filt_freq": filt_freq[l],
        "hy_conv_w": hy_conv_w[l], "hy_conv_b": hy_conv_b[l].reshape(1, -1), "hy_skip": hy_skip[l].reshape(2, 1, D_HYENA),
    }
    yh_ctx = _hyena_group(u3.reshape(T_ALL // SEQ, SEQ, c_u3), SEQ, BATCH, 0, LANE, lp)
    yh_lat = _hyena_group(u3.reshape(T_ALL // DEC_SEQ, DEC_SEQ, c_u3), DEC_SEQ, DEC_BATCH, T_CTX // DEC_SEQ, LANE, lp)

    wq = w_uq[l].reshape(Q_LORA, N_HEADS, QK_HEAD)
    wuq_p = jnp.concatenate([wq[..., :QK_NOPE].reshape(Q_LORA, -1),
                             _pad_lanes(wq[..., QK_NOPE:], LANE).reshape(Q_LORA, -1)], axis=-1).astype(BF16)
    wkv = w_ukv[l].reshape(KV_LORA, N_HEADS, QK_NOPE + V_HEAD)
    wukv_p = jnp.concatenate([wkv[..., :QK_NOPE].reshape(KV_LORA, -1),
                              wkv[..., QK_NOPE:].reshape(KV_LORA, -1)], axis=-1).astype(BF16)
    cos_t, sin_t = _rope_tables()
    q_all, ckv, kpe = _qkv(
        lat, q_a_norm_w[l].reshape(1, -1), wuq_p, jnp.tile(qn_norm_w[l], N_HEADS)[None, :],
        _pad_lanes(qr_norm_w[l], LANE)[None, :], kv_a_norm_w[l].reshape(1, -1),
        _pad_lanes(kr_norm_w[l], LANE)[None, :], cos_t, sin_t)
    kn_w = jnp.tile(kn_norm_w[l], N_HEADS)[None, :]
    k_ctx, v_ctx = _kvup(ckv[:T_CTX], wukv_p, kn_w)
    ckv_lat = jnp.concatenate([ckv[T_CTX:].reshape(DEC_BATCH, DEC_SEQ, KV_LORA), cache_ckv[:, l]], axis=1)
    k_lat, v_lat = _kvup(ckv_lat.reshape(DEC_BATCH * LK_LAT, KV_LORA), wukv_p, kn_w)
    kpe_lat = jnp.concatenate([kpe[T_CTX:].reshape(DEC_BATCH, DEC_SEQ, LANE),
                               _pad_lanes(cache_kpe[:, l], LANE)], axis=1)
    kpe_lat = kpe_lat.reshape(DEC_BATCH * LK_LAT, LANE).astype(BF16)
    ym_ctx = _attn_ctx(q_all, k_ctx, v_ctx, kpe[:T_CTX].astype(BF16))
    ym_lat = _attn_lat(q_all, k_lat, v_lat, kpe_lat)

    merged = _merge(yh_ctx, yh_lat, ym_ctx, ym_lat, gates, w_hy_out[l].astype(BF16), w_mla_out[l].astype(BF16))
    wr_t = w_router[l].T
    wr_hi = wr_t.astype(BF16)
    wr_lo = (wr_t - wr_hi.astype(F32)).astype(BF16)
    br = jnp.broadcast_to(b_router[l][:, None], (N_EXPERTS, LANE))
    x1, h2, logits_t = _oproj(merged, x_ctx, x_lat, w_o[l].astype(BF16), mod3, norm2_w[l], wr_hi, wr_lo, br)

    w_top, dest, counts, starts = _route(logits_t)
    counts = counts[:, 0]
    starts = starts[:, 0]
    dest_flat = dest.reshape(-1)
    xs = _dispatch(dest_flat, counts, starts, h2)
    e_tbl, row_tbl, ns_tbl, tail_tbl = _superblock_tables(counts)
    out_rows = _moe(e_tbl, row_tbl, ns_tbl, tail_tbl, xs, w_gate_up[l], b_gate_up[l], w_down[l], b_down[l])
    w_tok = w_top.T
    y_p = _combine(dest_flat, out_rows, x1, w_tok, mod3, 0, T_CTX).reshape(BATCH, SEQ, D_MODEL)
    y_s = _combine(dest_flat, out_rows, x1, w_tok, mod3, T_CTX, T_LAT).reshape(DEC_BATCH, DEC_SEQ, D_MODEL)
    new_ckv = ckv[:T_CTX].reshape(BATCH, 1, SEQ, KV_LORA)
    new_kpe = kpe[:T_CTX, :QK_ROPE].reshape(BATCH, 1, SEQ, QK_ROPE)
    return (y_p, y_s, new_ckv, new_kpe)
```
